```python
import math
import jax, jax.numpy as jnp
from jax import lax
import numpy as np

D_MODEL = 1024
BATCH = 16
SEQ = 2048
DEPTH = 1
DEC_BATCH = 128
DEC_SEQ = 1
PAST_LEN = 8192
PAGE_SIZE = 128

H_A = 8
KV_A = 4
G_A = H_A // KV_A
DH_A = 64
H_B = 16
KV_B = 8
G_B = H_B // KV_B
DH_B = 64
D_FF = 4 * D_MODEL
N_BUCKETS = 32
MAX_DISTANCE = 128
Q_BLOCK = 128
EPS = 1e-6
NEG = -1e30

SIZES = [H_A * 2 * DH_A,
         KV_A * 2 * DH_A,
         KV_A * 2 * DH_A,
         H_B * DH_B,
         KV_B * DH_B,
         KV_B * DH_B,
         H_B,
         D_MODEL,
         D_MODEL]
N_IN = int(sum(SIZES))
SPLITS = [int(s) for s in np.cumsum(SIZES)[:-1]]

kernel_name = "gated_diff_fox_hybrid_step"


def rms_norm(x, g):
    xf = x.astype(jnp.float32)
    y = xf * lax.rsqrt(jnp.mean(xf * xf, axis=-1, keepdims=True) + EPS)
    return (y * g.astype(jnp.float32)).astype(x.dtype)


def t5_bucket(dist):
    n = jnp.maximum(dist, 0)
    max_exact = N_BUCKETS // 2
    nf = jnp.maximum(n, max_exact).astype(jnp.float32)
    large = max_exact + (jnp.log(nf / max_exact) / math.log(MAX_DISTANCE / max_exact)
                         * (N_BUCKETS - max_exact)).astype(jnp.int32)
    large = jnp.minimum(large, N_BUCKETS - 1)
    return jnp.where(n < max_exact, n, large)


def project(xn, w_in, b_forget):
    B, T = xn.shape[:2]
    z = xn @ w_in
    qa, ka, va, qb, kb, vb, fl, ga, gb = jnp.split(z, SPLITS, axis=-1)
    qa = qa.reshape(B, T, H_A, 2, DH_A)
    ka = ka.reshape(B, T, KV_A, 2, DH_A)
    va = va.reshape(B, T, KV_A, 2 * DH_A)
    qb = qb.reshape(B, T, H_B, DH_B)
    kb = kb.reshape(B, T, KV_B, DH_B)
    vb = vb.reshape(B, T, KV_B, DH_B)
    logf = jax.nn.log_sigmoid((fl + b_forget).astype(jnp.float32))
    return qa, ka, va, qb, kb, vb, logf, ga, gb


def diff_attention(qa, ka, va, q_pos, k_pos, rel_bias, lam):
    B, Tq = qa.shape[:2]
    Tk = ka.shape[1]
    q = qa.reshape(B, Tq, KV_A, G_A, 2, DH_A)
    s = jnp.einsum('bqkgcd,bskcd->cbkgqs', q, ka).astype(jnp.float32) * (DH_A ** -0.5)
    dist = q_pos[:, None] - k_pos[None, :]
    bias = rel_bias[t5_bucket(dist)].astype(jnp.float32)
    bias = jnp.transpose(bias, (2, 0, 1)).reshape(KV_A, G_A, Tq, Tk)
    s = jnp.where(dist >= 0, s + bias, NEG)
    p = jax.nn.softmax(s, axis=-1)
    pd = p[0] - lam * p[1]
    o = jnp.einsum('bkgqs,bskd->bqkgd', pd.astype(va.dtype), va)
    return o.reshape(B, Tq, H_A, 2 * DH_A)


def forgetting_attention(qb, kb, vb, cq, ck, q_pos, k_pos):
    B, Tq = qb.shape[:2]
    Tk = kb.shape[1]
    q = qb.reshape(B, Tq, KV_B, G_B, DH_B)
    s = jnp.einsum('bqkgd,bskd->bkgqs', q, kb).astype(jnp.float32) * (DH_B ** -0.5)
    cqh = jnp.transpose(cq.astype(jnp.float32), (0, 2, 1)).reshape(B, KV_B, G_B, Tq)
    ckh = jnp.transpose(ck.astype(jnp.float32), (0, 2, 1)).reshape(B, KV_B, G_B, Tk)
    s = s + cqh[..., :, None] - ckh[..., None, :]
    mask = (q_pos[:, None] - k_pos[None, :]) >= 0
    s = jnp.where(mask, s, NEG)
    p = jax.nn.softmax(s, axis=-1)
    o = jnp.einsum('bkgqs,bskd->bqkgd', p.astype(vb.dtype), vb)
    return o.reshape(B, Tq, H_B, DH_B)


def prompt_attention(qa, ka, va, qb, kb, vb, logf, rel_bias, lam):
    B, S = qa.shape[:2]
    n_blocks = S // Q_BLOCK
    k_pos = jnp.arange(S)
    C = lax.cumsum(logf, axis=1)

    def block(i):
        start = i * Q_BLOCK
        q_pos = start + jnp.arange(Q_BLOCK)
        qa_i = lax.dynamic_slice_in_dim(qa, start, Q_BLOCK, axis=1)
        qb_i = lax.dynamic_slice_in_dim(qb, start, Q_BLOCK, axis=1)
        cq_i = lax.dynamic_slice_in_dim(C, start, Q_BLOCK, axis=1)
        oa = diff_attention(qa_i, ka, va, q_pos, k_pos, rel_bias, lam)
        ob = forgetting_attention(qb_i, kb, vb, cq_i, C, q_pos, k_pos)
        return oa, ob

    oa, ob = lax.map(block, jnp.arange(n_blocks))
    oa = jnp.moveaxis(oa, 0, 1).reshape(B, S, H_A, 2 * DH_A)
    ob = jnp.moveaxis(ob, 0, 1).reshape(B, S, H_B, DH_B)
    return oa, ob


def sample_attention(qa, ka, va, qb, kb, vb, logf, cache_a_k, cache_a_v, cache_b_k,
                     cache_b_v, cache_b_logf, page_table, rel_bias, lam):
    DB, T = qa.shape[:2]
    past_ka = cache_a_k[page_table].reshape(DB, -1, KV_A, 2, DH_A)
    past_va = cache_a_v[page_table].reshape(DB, -1, KV_A, 2 * DH_A)
    past_kb = cache_b_k[page_table].reshape(DB, -1, KV_B, DH_B)
    past_vb = cache_b_v[page_table].reshape(DB, -1, KV_B, DH_B)
    past_lf = cache_b_logf[page_table].reshape(DB, -1, H_B)
    P = past_ka.shape[1]
    ka_all = jnp.concatenate([past_ka.astype(ka.dtype), ka], axis=1)
    va_all = jnp.concatenate([past_va.astype(va.dtype), va], axis=1)
    kb_all = jnp.concatenate([past_kb.astype(kb.dtype), kb], axis=1)
    vb_all = jnp.concatenate([past_vb.astype(vb.dtype), vb], axis=1)
    lf_all = jnp.concatenate([past_lf.astype(jnp.float32), logf], axis=1)
    C = lax.cumsum(lf_all, axis=1)
    k_pos = jnp.arange(P + T)
    q_pos = P + jnp.arange(T)
    oa = diff_attention(qa, ka_all, va_all, q_pos, k_pos, rel_bias, lam)
    ob = forgetting_attention(qb, kb_all, vb_all, C[:, P:], C, q_pos, k_pos)
    return oa, ob


def merge_and_mlp(x, oa, ob, ga, gb, subln_g, lam_init, w_proj_a, w_proj_b, w_out,
                  norm_mlp_g, w_up, w_down):
    B, T = x.shape[:2]
    oa = rms_norm(oa, subln_g) * (1.0 - lam_init)
    pa = oa.reshape(B, T, H_A * 2 * DH_A) @ w_proj_a
    pb = ob.reshape(B, T, H_B * DH_B) @ w_proj_b
    h = jax.nn.sigmoid(ga) * pa + jax.nn.sigmoid(gb) * pb
    x = x + h @ w_out
    hn = rms_norm(x, norm_mlp_g)
    x = x + jnp.square(jax.nn.relu(hn @ w_up)) @ w_down
    return x


def setup_inputs(seed: int = 0) -> dict:
    key = jax.random.key(seed)
    ks = jax.random.split(key, 32)
    f32 = jnp.float32
    n_pages = PAST_LEN // PAGE_SIZE
    n_pool = (DEC_BATCH * n_pages * 5) // 4
    nrm = lambda k, shape, scale=1.0: jax.random.normal(k, shape, f32) * scale
    x_prompt = nrm(ks[0], (BATCH, SEQ, D_MODEL))
    x_sample = nrm(ks[1], (DEC_BATCH, DEC_SEQ, D_MODEL))
    cache_a_k = nrm(ks[2], (DEPTH, n_pool, PAGE_SIZE, KV_A, 2, DH_A))
    cache_a_v = nrm(ks[3], (DEPTH, n_pool, PAGE_SIZE, KV_A, 2 * DH_A))
    cache_b_k = nrm(ks[4], (DEPTH, n_pool, PAGE_SIZE, KV_B, DH_B))
    cache_b_v = nrm(ks[5], (DEPTH, n_pool, PAGE_SIZE, KV_B, DH_B))
    cache_b_logf = jax.nn.log_sigmoid(2.0 + nrm(ks[6], (DEPTH, n_pool, PAGE_SIZE, H_B)))
    page_table = jax.random.permutation(ks[7], n_pool)[:DEC_BATCH * n_pages]
    page_table = page_table.reshape(DEC_BATCH, n_pages).astype(jnp.int32)
    return {
        "x_prompt": x_prompt,
        "x_sample": x_sample,
        "cache_a_k": cache_a_k,
        "cache_a_v": cache_a_v,
        "cache_b_k": cache_b_k,
        "cache_b_v": cache_b_v,
        "cache_b_logf": cache_b_logf,
        "page_table": page_table,
        "norm_attn_g": 1.0 + nrm(ks[8], (DEPTH, D_MODEL), 0.05),
        "w_in": nrm(ks[9], (DEPTH, D_MODEL, N_IN), D_MODEL ** -0.5),
        "b_forget": 2.0 + nrm(ks[10], (DEPTH, H_B), 0.1),
        "rel_bias": nrm(ks[11], (N_BUCKETS, H_A), 0.5),
        "lam_q1": nrm(ks[12], (DEPTH, DH_A), 0.1),
        "lam_k1": nrm(ks[13], (DEPTH, DH_A), 0.1),
        "lam_q2": nrm(ks[14], (DEPTH, DH_A), 0.1),
        "lam_k2": nrm(ks[15], (DEPTH, DH_A), 0.1),
        "subln_g": 1.0 + nrm(ks[16], (DEPTH, 2 * DH_A), 0.05),
        "w_proj_a": nrm(ks[17], (DEPTH, H_A * 2 * DH_A, D_MODEL), (H_A * 2 * DH_A) ** -0.5),
        "w_proj_b": nrm(ks[18], (DEPTH, H_B * DH_B, D_MODEL), (H_B * DH_B) ** -0.5),
        "w_out": nrm(ks[19], (DEPTH, D_MODEL, D_MODEL), D_MODEL ** -0.5),
        "norm_mlp_g": 1.0 + nrm(ks[20], (DEPTH, D_MODEL), 0.05),
        "w_up": nrm(ks[21], (DEPTH, D_MODEL, D_FF), D_MODEL ** -0.5),
        "w_down": nrm(ks[22], (DEPTH, D_FF, D_MODEL), D_FF ** -0.5),
        "norm_final_g": 1.0 + nrm(ks[23], (D_MODEL,), 0.05),
    }


def reference(x_prompt, x_sample, cache_a_k, cache_a_v, cache_b_k, cache_b_v, cache_b_logf,
              page_table, norm_attn_g, w_in, b_forget, rel_bias, lam_q1, lam_k1, lam_q2,
              lam_k2, subln_g, w_proj_a, w_proj_b, w_out, norm_mlp_g, w_up, w_down,
              norm_final_g):
    xp, xs = x_prompt, x_sample
    pak, pav, pbk, pbv, pbf = [], [], [], [], []
    sak, sav, sbk, sbv, sbf = [], [], [], [], []
    for l in range(DEPTH):
        lam_init = 0.8 - 0.6 * math.exp(-0.3 * l)
        lam = (jnp.exp(jnp.sum(lam_q1[l].astype(jnp.float32) * lam_k1[l].astype(jnp.float32)))
               - jnp.exp(jnp.sum(lam_q2[l].astype(jnp.float32) * lam_k2[l].astype(jnp.float32)))
               + lam_init)
        xn = rms_norm(xp, norm_attn_g[l])
        qa, ka, va, qb, kb, vb, logf, ga, gb = project(xn, w_in[l], b_forget[l])
        oa, ob = prompt_attention(qa, ka, va, qb, kb, vb, logf, rel_bias, lam)
        xp = merge_and_mlp(xp, oa, ob, ga, gb, subln_g[l], lam_init, w_proj_a[l], w_proj_b[l],
                           w_out[l], norm_mlp_g[l], w_up[l], w_down[l])
        pak.append(ka); pav.append(va); pbk.append(kb); pbv.append(vb); pbf.append(logf)
        xn = rms_norm(xs, norm_attn_g[l])
        qa, ka, va, qb, kb, vb, logf, ga, gb = project(xn, w_in[l], b_forget[l])
        oa, ob = sample_attention(qa, ka, va, qb, kb, vb, logf, cache_a_k[l], cache_a_v[l],
                                  cache_b_k[l], cache_b_v[l], cache_b_logf[l], page_table,
                                  rel_bias, lam)
        xs = merge_and_mlp(xs, oa, ob, ga, gb, subln_g[l], lam_init, w_proj_a[l], w_proj_b[l],
                           w_out[l], norm_mlp_g[l], w_up[l], w_down[l])
        sak.append(ka); sav.append(va); sbk.append(kb); sbv.append(vb); sbf.append(logf)
    y_prompt = rms_norm(xp, norm_final_g)
    y_sample = rms_norm(xs, norm_final_g)
    return (y_prompt, y_sample,
            jnp.stack(pak, 0), jnp.stack(pav, 0), jnp.stack(pbk, 0), jnp.stack(pbv, 0),
            jnp.stack(pbf, 0),
            jnp.stack(sak, 0), jnp.stack(sav, 0), jnp.stack(sbk, 0), jnp.stack(sbv, 0),
            jnp.stack(sbf, 0))
```

```python
import functools
import math

import jax
import jax.numpy as jnp
import numpy as np
from jax import lax
from jax.experimental import pallas as pl
from jax.experimental.pallas import tpu as pltpu

F32 = jnp.float32
BF16 = jnp.bfloat16

D_MODEL = 1024
H_A, KV_A, DH_A = 8, 4, 64
G_A = H_A // KV_A
H_B, KV_B, DH_B = 16, 8, 64
G_B = H_B // KV_B
D_FF = 4 * D_MODEL
N_BUCKETS = 32
MAX_DISTANCE = 128
EPS = 1e-6
NEG = -1e30
LAM_INIT = 0.8 - 0.6 * math.exp(-0.3 * 0)
Q_SCALE = DH_A ** -0.5
assert DH_A == DH_B and G_A == 2 and G_B == 2

LANES = 128
VMEM_LIMIT_BYTES = 56 * 1024 * 1024

W_QA, W_KA, W_VA = H_A * 2 * DH_A, KV_A * 2 * DH_A, KV_A * 2 * DH_A
W_QB, W_KB, W_VB = H_B * DH_B, KV_B * DH_B, KV_B * DH_B
_OFFS = np.cumsum([0, W_QA, W_KA, W_VA, W_QB, W_KB, W_VB, D_MODEL, D_MODEL])
N_MAIN = int(_OFFS[-1])

_B_HEAD_PERM = np.array([4 * i + j for i in range(H_B // 4) for j in (0, 2, 1, 3)])
_B_HEAD_INV = np.argsort(_B_HEAD_PERM)


def _cparams(*sem):
    return pltpu.CompilerParams(dimension_semantics=sem, vmem_limit_bytes=VMEM_LIMIT_BYTES)


def _rms(x, g):
    return (x * lax.rsqrt(jnp.mean(x * x, axis=-1, keepdims=True) + EPS)) * g


def _t5_bucket(dist):
    n = jnp.maximum(dist, 0)
    max_exact = N_BUCKETS // 2
    nf = jnp.maximum(n, max_exact).astype(F32)
    large = max_exact + (jnp.log(nf / max_exact) / math.log(MAX_DISTANCE / max_exact)
                         * (N_BUCKETS - max_exact)).astype(jnp.int32)
    large = jnp.minimum(large, N_BUCKETS - 1)
    return jnp.where(n < max_exact, n, large)


def _split3(x):
    hi = x.astype(BF16)
    r1 = x - hi.astype(F32)
    mid = r1.astype(BF16)
    lo = (r1 - mid.astype(F32)).astype(BF16)
    return hi, mid, lo


def _dot(a, b):
    return jnp.dot(a, b, preferred_element_type=F32)


def _dot_nt(a, b):
    return lax.dot_general(a, b, (((1,), (1,)), ((), ())), preferred_element_type=F32)


def _in_proj_body(x_ref, g_ref, w_ref, wfl_ref, bfl_ref,
                  qa_ref, qb_ref, ga_ref, gb_ref, ka_ref, va_ref, kb_ref, vb_ref, lf_ref,
                  ka16_ref, va16_ref, kb16_ref, vb16_ref):
    xb = _rms(x_ref[...], g_ref[...]).astype(BF16)

    def seg(i):
        return _dot(xb, w_ref[:, int(_OFFS[i]):int(_OFFS[i + 1])])

    qa_ref[...] = (seg(0) * Q_SCALE).astype(BF16)
    for i, (o32, o16) in zip((1, 2, 4, 5), ((ka_ref, ka16_ref), (va_ref, va16_ref),
                                            (kb_ref, kb16_ref), (vb_ref, vb16_ref))):
        z = seg(i)
        o32[...] = z
        o16[...] = z.astype(BF16)
    qb_ref[...] = (seg(3) * Q_SCALE).astype(BF16)
    ga_ref[...] = (1.0 / (1.0 + jnp.exp(-seg(6)))).astype(BF16)
    gb_ref[...] = (1.0 / (1.0 + jnp.exp(-seg(7)))).astype(BF16)
    zf = _dot(xb, wfl_ref[...]) + bfl_ref[...]
    lf = jnp.minimum(zf, 0.0) - jnp.log1p(jnp.exp(-jnp.abs(zf)))
    lf_ref[...] = lf[:, :H_B]


def _in_proj(x2d, g, w_main, w_fl, b_fl, tm):
    rows, d = x2d.shape
    row = lambda n: pl.BlockSpec((tm, n), lambda i: (i, 0))
    const = lambda a: pl.BlockSpec(a.shape, lambda i: (0, 0))
    widths16 = (W_QA, W_QB, D_MODEL, D_MODEL)
    widths32 = (W_KA, W_VA, W_KB, W_VB)
    out_shape = ([jax.ShapeDtypeStruct((rows, n), BF16) for n in widths16]
                 + [jax.ShapeDtypeStruct((rows, n), F32) for n in widths32]
                 + [jax.ShapeDtypeStruct((rows, H_B), F32)]
                 + [jax.ShapeDtypeStruct((rows, n), BF16) for n in widths32])
    out_specs = ([row(n) for n in widths16] + [row(n) for n in widths32] + [row(H_B)]
                 + [row(n) for n in widths32])
    return pl.pallas_call(
        _in_proj_body, grid=(rows // tm,),
        in_specs=[row(d), const(g), const(w_main), const(w_fl), const(b_fl)],
        out_specs=out_specs, out_shape=out_shape,
        compiler_params=_cparams("arbitrary"), name="in_proj")(x2d, g, w_main, w_fl, b_fl)


_CS_BLK = 256


def _cumsum_body(lf_ref, c_ref):
    t = lf_ref.shape[2]
    r = lax.broadcasted_iota(jnp.int32, (_CS_BLK, _CS_BLK), 0)
    c = lax.broadcasted_iota(jnp.int32, (_CS_BLK, _CS_BLK), 1)
    tri = (r <= c).astype(F32).astype(BF16)
    carry = jnp.zeros((H_B, 1), F32)
    for i in range(t // _CS_BLK):
        sl = slice(i * _CS_BLK, (i + 1) * _CS_BLK)
        hi, mid, lo = _split3(lf_ref[0, :, sl])
        cs = _dot(hi, tri) + _dot(mid, tri) + _dot(lo, tri) + carry
        c_ref[0, :, sl] = cs
        carry = cs[:, _CS_BLK - 1:_CS_BLK]


def _cumsum_t(lf_t):
    b, h, t = lf_t.shape
    assert t % _CS_BLK == 0
    spec = pl.BlockSpec((1, h, t), lambda i: (i, 0, 0))
    return pl.pallas_call(
        _cumsum_body, grid=(b,), in_specs=[spec], out_specs=spec,
        out_shape=jax.ShapeDtypeStruct(lf_t.shape, F32),
        compiler_params=_cparams("arbitrary"), name="cumsum")(lf_t)


def _flash_init(m_ref, l_ref, acc_ref):
    m_ref[...] = jnp.full(m_ref.shape, -jnp.inf, F32)
    l_ref[...] = jnp.zeros(l_ref.shape, F32)
    acc_ref[...] = jnp.zeros(acc_ref.shape, F32)


def _flash_update(idx, s, v16, m_ref, l_ref, acc_ref):
    m_prev = m_ref[idx]
    m_new = jnp.maximum(m_prev, jnp.max(s, axis=-1, keepdims=True))
    alpha = jnp.exp(m_prev - m_new)
    p = jnp.exp(s - m_new)
    l_ref[idx] = alpha * l_ref[idx] + jnp.sum(p, axis=-1, keepdims=True)
    acc_ref[idx] = alpha * acc_ref[idx] + _dot(p.astype(BF16), v16)
    m_ref[idx] = m_new


def _lam(lq1_ref, lk1_ref, lq2_ref, lk2_ref):
    return (jnp.exp(jnp.sum(lq1_ref[...] * lk1_ref[...], keepdims=True))
            - jnp.exp(jnp.sum(lq2_ref[...] * lk2_ref[...], keepdims=True)) + LAM_INIT)


def _attn_a_body(rb_ref, q_ref, k_ref, v_ref, lq1_ref, lk1_ref, lq2_ref, lk2_ref, sg_ref,
                 o_ref, bias_ref, bucket_ref, qm_ref, m_ref, l_ref, acc_ref, *, tq):
    b, kvh, qi = pl.program_id(0), pl.program_id(1), pl.program_id(2)

    @pl.when((b == 0) & (kvh == 0) & (qi == 0))
    def _():
        r = lax.broadcasted_iota(jnp.int32, (tq, tq), 0)
        c = lax.broadcasted_iota(jnp.int32, (tq, tq), 1)
        for t in range(2):
            dist = r - c + t * tq
            bucket_ref[...] = _t5_bucket(dist)
            for h in range(H_A):
                bias_ref[h, t] = jnp.zeros((tq, tq), F32)

                def body(i, carry, h=h, t=t):
                    bias_ref[h, t] = jnp.where(bucket_ref[...] == i, rb_ref[i * H_A + h], bias_ref[h, t])
                    return carry

                lax.fori_loop(0, N_BUCKETS, body, 0)
                if t == 0:
                    bias_ref[h, t] = jnp.where(dist >= 0, bias_ref[h, t], NEG)

    lane = lax.broadcasted_iota(jnp.int32, (tq, LANES), 1)
    for g in range(G_A):
        qg = q_ref[0, :, g * LANES:(g + 1) * LANES].astype(F32)
        for c in range(2):
            keep = (lane >= DH_A) if c else (lane < DH_A)
            qm_ref[g * 2 + c] = jnp.where(keep, qg, 0.0).astype(BF16)
    _flash_init(m_ref, l_ref, acc_ref)

    def step(kb, kind):
        rows = pl.ds(pl.multiple_of(kb * tq, tq), tq)
        k16 = k_ref[0, rows, :]
        v16 = v_ref[0, rows, :]
        for g in range(G_A):
            h = kvh * G_A + g
            for c in range(2):
                idx = g * 2 + c
                s = _dot_nt(qm_ref[idx], k16)
                if kind == "far":
                    s = s + rb_ref[(N_BUCKETS - 1) * H_A + h]
                else:
                    s = s + bias_ref[h, 1 if kind == "sub" else 0]
                _flash_update(idx, s, v16, m_ref, l_ref, acc_ref)

    def far_body(kb, carry):
        step(kb, "far")
        return carry

    lax.fori_loop(0, jnp.maximum(qi - 1, 0), far_body, 0)

    @pl.when(qi >= 1)
    def _():
        step(qi - 1, "sub")

    step(qi, "diag")

    lam = _lam(lq1_ref, lk1_ref, lq2_ref, lk2_ref)
    for g in range(G_A):
        o = acc_ref[2 * g] / l_ref[2 * g] - lam * (acc_ref[2 * g + 1] / l_ref[2 * g + 1])
        y = _rms(o, sg_ref[...]) * (1.0 - LAM_INIT)
        o_ref[0, :, g * LANES:(g + 1) * LANES] = y.astype(BF16)


def _attn_a(rb_flat, qa, ka16, va16, lq1, lk1, lq2, lk2, sg, tq):
    b, t, _ = qa.shape
    assert t % tq == 0 and tq >= MAX_DISTANCE
    wq = G_A * 2 * DH_A
    qspec = pl.BlockSpec((1, tq, wq), lambda bi, k, i: (bi, i, k))
    kvspec = pl.BlockSpec((1, t, 2 * DH_A), lambda bi, k, i: (bi, 0, k))
    small = lambda a: pl.BlockSpec(a.shape, lambda bi, k, i: (0, 0))
    return pl.pallas_call(
        functools.partial(_attn_a_body, tq=tq), grid=(b, KV_A, t // tq),
        in_specs=[pl.BlockSpec(memory_space=pltpu.SMEM), qspec, kvspec, kvspec,
                  small(lq1), small(lk1), small(lq2), small(lk2), small(sg)],
        out_specs=qspec, out_shape=jax.ShapeDtypeStruct(qa.shape, BF16),
        scratch_shapes=[pltpu.VMEM((H_A, 2, tq, tq), F32), pltpu.VMEM((tq, tq), jnp.int32),
                        pltpu.VMEM((4, tq, LANES), BF16), pltpu.VMEM((4, tq, 1), F32),
                        pltpu.VMEM((4, tq, 1), F32), pltpu.VMEM((4, tq, LANES), F32)],
        compiler_params=_cparams("arbitrary", "arbitrary", "arbitrary"), name="attn_a",
    )(rb_flat, qa, ka16, va16, lq1, lk1, lq2, lk2, sg)


def _attn_b_body(q_ref, k_ref, v_ref, cq_ref, ck_ref, o_ref, qm_ref, m_ref, l_ref, acc_ref, *, tq):
    qi = pl.program_id(2)
    lane = lax.broadcasted_iota(jnp.int32, (tq, LANES), 1)
    for s_ in range(2):
        qs = q_ref[0, :, s_ * LANES:(s_ + 1) * LANES].astype(F32)
        for half in range(2):
            keep = (lane >= DH_B) if half else (lane < DH_B)
            qm_ref[2 * s_ + half] = jnp.where(keep, qs, 0.0).astype(BF16)
    _flash_init(m_ref, l_ref, acc_ref)

    def step(kb, diag):
        rows = pl.ds(pl.multiple_of(kb * tq, tq), tq)
        k16 = k_ref[0, rows, :]
        v16 = v_ref[0, rows, :]
        if diag:
            r = lax.broadcasted_iota(jnp.int32, (tq, tq), 0)
            c = lax.broadcasted_iota(jnp.int32, (tq, tq), 1)
            causal = r >= c
        for s_ in range(2):
            for half in range(2):
                idx = 2 * s_ + half
                j = 2 * half + s_
                s = _dot_nt(qm_ref[idx], k16)
                s = s + cq_ref[0, 0, :, j:j + 1] - ck_ref[0, 0, kb, j:j + 1, :]
                if diag:
                    s = jnp.where(causal, s, NEG)
                _flash_update(idx, s, v16, m_ref, l_ref, acc_ref)

    def body(kb, carry):
        step(kb, False)
        return carry

    lax.fori_loop(0, qi, body, 0)
    step(qi, True)

    for s_ in range(2):
        o0 = acc_ref[2 * s_] / l_ref[2 * s_]
        o1 = acc_ref[2 * s_ + 1] / l_ref[2 * s_ + 1]
        o_ref[0, :, s_ * LANES:(s_ + 1) * LANES] = jnp.where(lane < DH_B, o0, o1).astype(BF16)


def _attn_b(qb, kb16, vb16, cq4, ck4, tq):
    b, t, _ = qb.shape
    npair = KV_B // 2
    qspec = pl.BlockSpec((1, tq, 2 * LANES), lambda bi, k, i: (bi, i, k))
    kvspec = pl.BlockSpec((1, t, LANES), lambda bi, k, i: (bi, 0, k))
    cqspec = pl.BlockSpec((1, 1, tq, 4), lambda bi, k, i: (bi, k, i, 0))
    ckspec = pl.BlockSpec((1, 1, t // tq, 4, tq), lambda bi, k, i: (bi, k, 0, 0, 0))
    return pl.pallas_call(
        functools.partial(_attn_b_body, tq=tq), grid=(b, npair, t // tq),
        in_specs=[qspec, kvspec, kvspec, cqspec, ckspec],
        out_specs=qspec, out_shape=jax.ShapeDtypeStruct(qb.shape, BF16),
        scratch_shapes=[pltpu.VMEM((4, tq, LANES), BF16), pltpu.VMEM((4, tq, 1), F32),
                        pltpu.VMEM((4, tq, 1), F32), pltpu.VMEM((4, tq, LANES), F32)],
        compiler_params=_cparams("arbitrary", "arbitrary", "arbitrary"), name="attn_b",
    )(qb, kb16, vb16, cq4, ck4)


def _decode_body(pt_ref, wqa_ref, wqb_ref, kan_ref, van_ref, kbn_ref, vbn_ref, lfn_ref, rbt_ref,
                 cak_ref, cav_ref, cbk_ref, cbv_ref, clf_ref,
                 lq1_ref, lk1_ref, lq2_ref, lk2_ref, sg_ref,
                 oa_ref, ob_ref,
                 bias_ref, m_ref, l_ref, acc_ref, carry_ref, *, page):
    del pt_ref
    p = pl.program_id(1)
    nrow = 2 * H_A

    @pl.when(p == 0)
    def _():
        sa = jnp.sum(wqa_ref[0].astype(F32) * kan_ref[0], axis=-1, keepdims=True) + rbt_ref[:, 0:1]
        sb = jnp.sum(wqb_ref[0].astype(F32) * kbn_ref[0], axis=-1, keepdims=True)
        m_ref[0] = sa
        m_ref[1] = sb
        l_ref[...] = jnp.ones(l_ref.shape, F32)
        acc_ref[0] = jnp.broadcast_to(van_ref[0], acc_ref.shape[1:])
        acc_ref[1] = jnp.broadcast_to(vbn_ref[0], acc_ref.shape[1:])
        carry_ref[...] = lfn_ref[0]
        dist = page - lax.broadcasted_iota(jnp.int32, (nrow, page), 1)
        bucket = _t5_bucket(dist)
        acc = jnp.zeros((nrow, page), F32)
        for i in range(N_BUCKETS):
            acc = jnp.where(bucket == i, rbt_ref[:, i:i + 1], acc)
        bias_ref[...] = acc

    @pl.when(p == 1)
    def _():
        bias_ref[...] = jnp.broadcast_to(rbt_ref[:, N_BUCKETS - 1:N_BUCKETS], bias_ref.shape)

    s_a = _dot_nt(wqa_ref[0], cak_ref[0].astype(BF16)) + bias_ref[...]
    _flash_update(0, s_a, cav_ref[0].astype(BF16), m_ref, l_ref, acc_ref)

    lf_t = clf_ref[0]
    r = lax.broadcasted_iota(jnp.int32, (page, page), 0)
    c = lax.broadcasted_iota(jnp.int32, (page, page), 1)
    later = (r > c).astype(F32).astype(BF16)
    suf3 = _dot(jnp.concatenate(_split3(lf_t), axis=0), later)
    suffix = suf3[0:H_B] + suf3[H_B:2 * H_B] + suf3[2 * H_B:3 * H_B]
    carry = carry_ref[...]
    s_b = _dot_nt(wqb_ref[0], cbk_ref[0].astype(BF16)) + (suffix + carry)
    carry_ref[...] = carry + suffix[:, 0:1] + lf_t[:, 0:1]
    _flash_update(1, s_b, cbv_ref[0].astype(BF16), m_ref, l_ref, acc_ref)

    @pl.when(p == pl.num_programs(1) - 1)
    def _():
        rowi = lax.broadcasted_iota(jnp.int32, acc_ref.shape[1:], 0)
        lanei = lax.broadcasted_iota(jnp.int32, acc_ref.shape[1:], 1)

        def own_block(o, keep):
            o = jnp.where(keep, o, 0.0)
            return sum(o[:, j * LANES:(j + 1) * LANES] for j in range(o.shape[1] // LANES))

        oa = own_block(acc_ref[0] / l_ref[0], (lanei // (2 * DH_A)) == ((rowi % H_A) // G_A))
        lam = _lam(lq1_ref, lk1_ref, lq2_ref, lk2_ref)
        o = oa[0:H_A] - lam * oa[H_A:2 * H_A]
        oa_ref[0] = _rms(o, sg_ref[...]) * (1.0 - LAM_INIT)
        ob_ref[0] = own_block(acc_ref[1] / l_ref[1], (lanei // DH_B) == (rowi // G_B))


def _decode(page_table, wqa, wqb, ka_new, va_new, kb_new, vb_new, lf_new, rb_t,
            cak, cav, cbk, cbv, clf_t, lq1, lk1, lq2, lk2, sg):
    db, n_pages = page_table.shape
    page = cak.shape[1]
    assert page >= MAX_DISTANCE and page == LANES
    per_b = lambda a: pl.BlockSpec((1,) + a.shape[1:], lambda b, p, pt: (b, 0, 0))
    small = lambda a: pl.BlockSpec(a.shape, lambda b, p, pt: (0, 0))
    paged = lambda a: pl.BlockSpec((1,) + a.shape[1:],
                                   lambda b, p, pt: (pt[b * n_pages + n_pages - 1 - p], 0, 0))
    nrow = 2 * H_A
    grid_spec = pltpu.PrefetchScalarGridSpec(
        num_scalar_prefetch=1, grid=(db, n_pages),
        in_specs=[per_b(wqa), per_b(wqb), per_b(ka_new), per_b(va_new), per_b(kb_new), per_b(vb_new),
                  per_b(lf_new), small(rb_t),
                  paged(cak), paged(cav), paged(cbk), paged(cbv), paged(clf_t),
                  small(lq1), small(lk1), small(lq2), small(lk2), small(sg)],
        out_specs=[pl.BlockSpec((1, H_A, 2 * DH_A), lambda b, p, pt: (b, 0, 0)),
                   pl.BlockSpec((1, H_B, LANES), lambda b, p, pt: (b, 0, 0))],
        scratch_shapes=[pltpu.VMEM((nrow, page), F32), pltpu.VMEM((2, nrow, 1), F32),
                        pltpu.VMEM((2, nrow, 1), F32), pltpu.VMEM((2, nrow, cak.shape[2]), F32),
                        pltpu.VMEM((H_B, 1), F32)])
    return pl.pallas_call(
        functools.partial(_decode_body, page=page), grid_spec=grid_spec,
        out_shape=[jax.ShapeDtypeStruct((db, H_A, 2 * DH_A), F32),
                   jax.ShapeDtypeStruct((db, H_B, LANES), F32)],
        compiler_params=_cparams("arbitrary", "arbitrary"), name="decode",
    )(page_table.reshape(-1), wqa, wqb, ka_new, va_new, kb_new, vb_new, lf_new, rb_t,
      cak, cav, cbk, cbv, clf_t, lq1, lk1, lq2, lk2, sg)


def _merge_body(x_ref, oa_ref, ob_ref, ga_ref, gb_ref, wpa_ref, wpb_ref, wo_ref, x1_ref):
    pa = _dot(oa_ref[...], wpa_ref[...])
    pb = _dot(ob_ref[...], wpb_ref[...])
    h = ga_ref[...].astype(F32) * pa + gb_ref[...].astype(F32) * pb
    x1_ref[...] = x_ref[...] + _dot(h.astype(BF16), wo_ref[...])


def _merge(x2d, oa, ob, ga, gb, wpa, wpb, wo, tm):
    rows, d = x2d.shape
    row = pl.BlockSpec((tm, d), lambda i: (i, 0))
    const = lambda a: pl.BlockSpec(a.shape, lambda i: (0, 0))
    return pl.pallas_call(
        _merge_body, grid=(rows // tm,),
        in_specs=[row, row, row, row, row, const(wpa), const(wpb), const(wo)],
        out_specs=row, out_shape=jax.ShapeDtypeStruct(x2d.shape, F32),
        compiler_params=_cparams("arbitrary"), name="merge")(x2d, oa, ob, ga, gb, wpa, wpb, wo)


_FF_CHUNK = 1024


def _mlp_body(x_ref, g_ref, wu_ref, wd_ref, gf_ref, y_ref):
    x = x_ref[...]
    hn = _rms(x, g_ref[...]).astype(BF16)
    acc = x
    for c in range(D_FF // _FF_CHUNK):
        sl = slice(c * _FF_CHUNK, (c + 1) * _FF_CHUNK)
        u = jnp.maximum(_dot(hn, wu_ref[:, sl]), 0.0)
        acc = acc + _dot((u * u).astype(BF16), wd_ref[sl, :])
    y_ref[...] = _rms(acc, gf_ref[...])


def _mlp(x2d, g, wu, wd, gf, tm):
    rows, d = x2d.shape
    row = pl.BlockSpec((tm, d), lambda i: (i, 0))
    const = lambda a: pl.BlockSpec(a.shape, lambda i: (0, 0))
    return pl.pallas_call(
        _mlp_body, grid=(rows // tm,),
        in_specs=[row, const(g), const(wu), const(wd), const(gf)],
        out_specs=row, out_shape=jax.ShapeDtypeStruct(x2d.shape, F32),
        compiler_params=_cparams("arbitrary"), name="mlp")(x2d, g, wu, wd, gf)


def _row_tile(rows, want):
    return want if rows % want == 0 else rows


def kernel(x_prompt, x_sample, cache_a_k, cache_a_v, cache_b_k, cache_b_v, cache_b_logf, page_table,
           norm_attn_g, w_in, b_forget, rel_bias, lam_q1, lam_k1, lam_q2, lam_k2, subln_g,
           w_proj_a, w_proj_b, w_out, norm_mlp_g, w_up, w_down, norm_final_g):
    depth = w_in.shape[0]
    assert depth == 1, "one layer supported"
    B, T, D = x_prompt.shape
    DB, TS, _ = x_sample.shape
    assert TS == 1 and D == D_MODEL
    n_pool, page = cache_a_k.shape[1], cache_a_k.shape[2]

    w = w_in[0]
    splits = np.cumsum([W_QA, W_KA, W_VA, W_QB, W_KB, W_VB, H_B, D_MODEL])
    w_qa, w_ka, w_va, w_qb, w_kb, w_vb, w_f, w_ga, w_gb = jnp.split(w, [int(s) for s in splits], axis=1)
    w_qb = w_qb.reshape(D, H_B, DH_B)[:, _B_HEAD_PERM].reshape(D, W_QB)
    w_main = jnp.concatenate([w_qa, w_ka, w_va, w_qb, w_kb, w_vb, w_ga, w_gb], axis=1).astype(BF16)
    w_fl = jnp.pad(w_f, ((0, 0), (0, LANES - H_B))).astype(BF16)
    b_fl = jnp.pad(b_forget[0].astype(F32), (0, LANES - H_B)).reshape(1, LANES)
    g_attn = norm_attn_g[0].reshape(1, D).astype(F32)
    wpa = w_proj_a[0].astype(BF16)
    wpb = w_proj_b[0].reshape(H_B, DH_B, D)[_B_HEAD_PERM].reshape(H_B * DH_B, D).astype(BF16)
    wo = w_out[0].astype(BF16)
    wu = w_up[0].astype(BF16)
    wd = w_down[0].astype(BF16)
    g_mlp = norm_mlp_g[0].reshape(1, D).astype(F32)
    g_fin = norm_final_g.reshape(1, D).astype(F32)
    lq1, lk1, lq2, lk2 = (a[0].reshape(1, DH_A).astype(F32) for a in (lam_q1, lam_k1, lam_q2, lam_k2))
    sg = subln_g[0].reshape(1, 2 * DH_A).astype(F32)
    rb = rel_bias.astype(F32)

    xp = x_prompt.reshape(B * T, D)
    tm = _row_tile(B * T, 256)
    (qa, qb, ga, gb, ka, va, kb, vb, lf, ka16, va16, kb16, vb16) = _in_proj(xp, g_attn, w_main, w_fl, b_fl, tm)
    c_t = _cumsum_t(lf.reshape(B, T, H_B).transpose(0, 2, 1))
    tq = _row_tile(T, 256)
    c4 = c_t.reshape(B, H_B // 4, 4, T)
    cq4 = c4.transpose(0, 1, 3, 2)
    ck4 = c4.reshape(B, H_B // 4, 4, T // tq, tq).transpose(0, 1, 3, 2, 4)
    oa = _attn_a(rb.reshape(-1), qa.reshape(B, T, W_QA), ka16.reshape(B, T, W_KA), va16.reshape(B, T, W_VA),
                 lq1, lk1, lq2, lk2, sg, tq)
    ob = _attn_b(qb.reshape(B, T, W_QB), kb16.reshape(B, T, W_KB), vb16.reshape(B, T, W_VB), cq4, ck4, tq)
    x1 = _merge(xp, oa.reshape(B * T, W_QA), ob.reshape(B * T, W_QB), ga, gb, wpa, wpb, wo, tm)
    y_prompt = _mlp(x1, g_mlp, wu, wd, g_fin, tm).reshape(B, T, D)

    xs = x_sample.reshape(DB, D)
    tms = _row_tile(DB, 128)
    (qa_s, qb_s, ga_s, gb_s, ka_s, va_s, kb_s, vb_s, lf_s, _, _, _, _) = _in_proj(xs, g_attn, w_main, w_fl, b_fl, tms)
    qa5 = qa_s.reshape(DB, KV_A, G_A, 2, DH_A)
    wqa = jnp.einsum("bkgcd,kK,cC->bckgKCd", qa5, jnp.eye(KV_A, dtype=BF16), jnp.eye(2, dtype=BF16))
    wqa = wqa.reshape(DB, 2 * H_A, W_KA)
    qb4 = qb_s.reshape(DB, H_B, DH_B)[:, _B_HEAD_INV].reshape(DB, KV_B, G_B, DH_B)
    wqb = jnp.einsum("bkgd,kK->bkgKd", qb4, jnp.eye(KV_B, dtype=BF16)).reshape(DB, H_B, W_KB)
    rb_t = jnp.tile(rb.T, (2, 1))
    oa_s, ob_raw = _decode(
        page_table, wqa, wqb, ka_s.reshape(DB, 1, W_KA), va_s.reshape(DB, 1, W_VA),
        kb_s.reshape(DB, 1, W_KB), vb_s.reshape(DB, 1, W_VB), lf_s.reshape(DB, H_B, 1), rb_t,
        cache_a_k[0].reshape(n_pool, page, W_KA), cache_a_v[0].reshape(n_pool, page, W_VA),
        cache_b_k[0].reshape(n_pool, page, W_KB), cache_b_v[0].reshape(n_pool, page, W_VB),
        cache_b_logf[0].transpose(0, 2, 1), lq1, lk1, lq2, lk2, sg)
    odd_kv = ((jnp.arange(H_B) // G_B) % 2 == 1)[None, :, None]
    ob_s = jnp.where(odd_kv, ob_raw[..., DH_B:], ob_raw[..., :DH_B])
    ob_s = ob_s[:, _B_HEAD_PERM].reshape(DB, W_QB).astype(BF16)
    x1_s = _merge(xs, oa_s.reshape(DB, W_QA).astype(BF16), ob_s, ga_s, gb_s, wpa, wpb, wo, tms)
    y_sample = _mlp(x1_s, g_mlp, wu, wd, g_fin, tms).reshape(DB, 1, D)

    return (y_prompt, y_sample,
            ka.reshape(1, B, T, KV_A, 2, DH_A), va.reshape(1, B, T, KV_A, 2 * DH_A),
            kb.reshape(1, B, T, KV_B, DH_B), vb.reshape(1, B, T, KV_B, DH_B), lf.reshape(1, B, T, H_B),
            ka_s.reshape(1, DB, 1, KV_A, 2, DH_A), va_s.reshape(1, DB, 1, KV_A, 2 * DH_A),
            kb_s.reshape(1, DB, 1, KV_B, DH_B), vb_s.reshape(1, DB, 1, KV_B, DH_B),
            lf_s.reshape(1, DB, 1, H_B))
```

```python
import functools
import math

import jax
import jax.numpy as jnp
import numpy as np
from jax import lax
from jax.experimental import pallas as pl
from jax.experimental.pallas import tpu as pltpu

F32 = jnp.float32
BF16 = jnp.bfloat16

D_MODEL = 1024
H_A, KV_A, DH_A = 8, 4, 64
G_A = H_A // KV_A
H_B, KV_B, DH_B = 16, 8, 64
G_B = H_B // KV_B
D_FF = 4 * D_MODEL
N_BUCKETS = 32
MAX_DISTANCE = 128
EPS = 1e-6
NEG = -1e30
LAM_INIT = 0.8 - 0.6 * math.exp(-0.3 * 0)
Q_SCALE = DH_A ** -0.5
assert DH_A == DH_B and G_A == 2 and G_B == 2

LANES = 128
VMEM_LIMIT_BYTES = 56 * 1024 * 1024

W_QA, W_KA, W_VA = H_A * 2 * DH_A, KV_A * 2 * DH_A, KV_A * 2 * DH_A
W_QB, W_KB, W_VB = H_B * DH_B, KV_B * DH_B, KV_B * DH_B
_OFFS = np.cumsum([0, W_QA, W_KA, W_VA, W_QB, W_KB, W_VB, D_MODEL, D_MODEL])
N_MAIN = int(_OFFS[-1])

_B_HEAD_PERM = np.array([4 * i + j for i in range(H_B // 4) for j in (0, 2, 1, 3)])
_B_HEAD_INV = np.argsort(_B_HEAD_PERM)


def _cparams(*sem):
    return pltpu.CompilerParams(dimension_semantics=sem, vmem_limit_bytes=VMEM_LIMIT_BYTES)


def _rms(x, g):
    return (x * lax.rsqrt(jnp.mean(x * x, axis=-1, keepdims=True) + EPS)) * g


def _t5_bucket(dist):
    n = jnp.maximum(dist, 0)
    max_exact = N_BUCKETS // 2
    nf = jnp.maximum(n, max_exact).astype(F32)
    large = max_exact + (jnp.log(nf / max_exact) / math.log(MAX_DISTANCE / max_exact)
                         * (N_BUCKETS - max_exact)).astype(jnp.int32)
    large = jnp.minimum(large, N_BUCKETS - 1)
    return jnp.where(n < max_exact, n, large)


def _split3(x):
    hi = x.astype(BF16)
    r1 = x - hi.astype(F32)
    mid = r1.astype(BF16)
    lo = (r1 - mid.astype(F32)).astype(BF16)
    return hi, mid, lo


def _dot(a, b):
    return jnp.dot(a, b, preferred_element_type=F32)


def _dot_nt(a, b):
    return lax.dot_general(a, b, (((1,), (1,)), ((), ())), preferred_element_type=F32)


def _in_proj_body(x_ref, g_ref, w_ref, wfl_ref, bfl_ref,
                  qa_ref, qb_ref, ga_ref, gb_ref, ka_ref, va_ref, kb_ref, vb_ref, lf_ref,
                  ka16_ref, va16_ref, kb16_ref, vb16_ref):
    xb = _rms(x_ref[...], g_ref[...]).astype(BF16)

    def seg(i):
        return _dot(xb, w_ref[:, int(_OFFS[i]):int(_OFFS[i + 1])])

    qa_ref[...] = (seg(0) * Q_SCALE).astype(BF16)
    for i, (o32, o16) in zip((1, 2, 4, 5), ((ka_ref, ka16_ref), (va_ref, va16_ref),
                                            (kb_ref, kb16_ref), (vb_ref, vb16_ref))):
        z = seg(i)
        o32[...] = z
        o16[...] = z.astype(BF16)
    qb_ref[...] = (seg(3) * Q_SCALE).astype(BF16)
    ga_ref[...] = (1.0 / (1.0 + jnp.exp(-seg(6)))).astype(BF16)
    gb_ref[...] = (1.0 / (1.0 + jnp.exp(-seg(7)))).astype(BF16)
    zf = _dot(xb, wfl_ref[...]) + bfl_ref[...]
    lf = jnp.minimum(zf, 0.0) - jnp.log1p(jnp.exp(-jnp.abs(zf)))
    lf_ref[...] = lf[:, :H_B]


def _in_proj(x2d, g, w_main, w_fl, b_fl, tm):
    rows, d = x2d.shape
    row = lambda n: pl.BlockSpec((tm, n), lambda i: (i, 0))
    const = lambda a: pl.BlockSpec(a.shape, lambda i: (0, 0))
    widths16 = (W_QA, W_QB, D_MODEL, D_MODEL)
    widths32 = (W_KA, W_VA, W_KB, W_VB)
    out_shape = ([jax.ShapeDtypeStruct((rows, n), BF16) for n in widths16]
                 + [jax.ShapeDtypeStruct((rows, n), F32) for n in widths32]
                 + [jax.ShapeDtypeStruct((rows, H_B), F32)]
                 + [jax.ShapeDtypeStruct((rows, n), BF16) for n in widths32])
    out_specs = ([row(n) for n in widths16] + [row(n) for n in widths32] + [row(H_B)]
                 + [row(n) for n in widths32])
    return pl.pallas_call(
        _in_proj_body, grid=(rows // tm,),
        in_specs=[row(d), const(g), const(w_main), const(w_fl), const(b_fl)],
        out_specs=out_specs, out_shape=out_shape,
        compiler_params=_cparams("arbitrary"), name="in_proj")(x2d, g, w_main, w_fl, b_fl)


_CS_BLK = 256


def _cumsum_body(lf_ref, c_ref):
    t = lf_ref.shape[2]
    r = lax.broadcasted_iota(jnp.int32, (_CS_BLK, _CS_BLK), 0)
    c = lax.broadcasted_iota(jnp.int32, (_CS_BLK, _CS_BLK), 1)
    tri = (r <= c).astype(F32).astype(BF16)
    carry = jnp.zeros((H_B, 1), F32)
    for i in range(t // _CS_BLK):
        sl = slice(i * _CS_BLK, (i + 1) * _CS_BLK)
        hi, mid, lo = _split3(lf_ref[0, :, sl])
        cs = _dot(hi, tri) + _dot(mid, tri) + _dot(lo, tri) + carry
        c_ref[0, :, sl] = cs
        carry = cs[:, _CS_BLK - 1:_CS_BLK]


def _cumsum_t(lf_t):
    b, h, t = lf_t.shape
    assert t % _CS_BLK == 0
    spec = pl.BlockSpec((1, h, t), lambda i: (i, 0, 0))
    return pl.pallas_call(
        _cumsum_body, grid=(b,), in_specs=[spec], out_specs=spec,
        out_shape=jax.ShapeDtypeStruct(lf_t.shape, F32),
        compiler_params=_cparams("arbitrary"), name="cumsum")(lf_t)


def _lam(lq1_ref, lk1_ref, lq2_ref, lk2_ref):
    return (jnp.exp(jnp.sum(lq1_ref[...] * lk1_ref[...], keepdims=True))
            - jnp.exp(jnp.sum(lq2_ref[...] * lk2_ref[...], keepdims=True)) + LAM_INIT)


def _lane_chunks(x):
    return [x[:, j * LANES:(j + 1) * LANES] for j in range(x.shape[1] // LANES)]


def _score_pass_tail(kb, s, s_ref, mpart_ref):
    s_ref[kb] = s
    mp = mpart_ref[...]
    for sj in _lane_chunks(s):
        mp = jnp.maximum(mp, sj)
    mpart_ref[...] = mp


def _value_pass(n_blocks, tq, s_ref, mb_ref, v_ref, lpart_ref, acc_ref):
    lpart_ref[...] = jnp.zeros(lpart_ref.shape, F32)
    acc_ref[...] = jnp.zeros(acc_ref.shape, F32)

    def body(kb, carry):
        mb = mb_ref[...]
        ps = [jnp.exp(sj - mb) for sj in _lane_chunks(s_ref[kb])]
        lpart_ref[...] += functools.reduce(lambda a, b: a + b, ps)
        p16 = jnp.concatenate(ps, axis=1).astype(BF16)
        rows = pl.ds(pl.multiple_of(kb * tq, tq), tq)
        acc_ref[...] += _dot(p16, v_ref[0, rows, :])
        return carry

    lax.fori_loop(0, n_blocks, body, 0)
    return acc_ref[...] / jnp.sum(lpart_ref[...], axis=-1, keepdims=True)


def _attn_a_body(rb_ref, q_ref, k_ref, v_ref, lq1_ref, lk1_ref, lq2_ref, lk2_ref, sg_ref,
                 o_ref, bias_ref, bucket_ref, q4_ref, s_ref, mpart_ref, mb_ref, lpart_ref, acc_ref, *, tq):
    b, kvh, qi = pl.program_id(0), pl.program_id(1), pl.program_id(2)
    far = (N_BUCKETS - 1) * H_A

    @pl.when((b == 0) & (kvh == 0) & (qi == 0))
    def _():
        r = lax.broadcasted_iota(jnp.int32, (tq, tq), 0)
        c = lax.broadcasted_iota(jnp.int32, (tq, tq), 1)
        for t in range(2):
            dist = r - c + t * tq
            bucket_ref[...] = _t5_bucket(dist)
            for h in range(H_A):
                bias_ref[h, t] = jnp.zeros((tq, tq), F32)

                def body(i, carry, h=h, t=t):
                    rel = rb_ref[i * H_A + h] - rb_ref[far + h]
                    bias_ref[h, t] = jnp.where(bucket_ref[...] == i, rel, bias_ref[h, t])
                    return carry

                lax.fori_loop(0, N_BUCKETS, body, 0)
                if t == 0:
                    bias_ref[h, t] = jnp.where(dist >= 0, bias_ref[h, t], NEG)

    lane = lax.broadcasted_iota(jnp.int32, (tq, LANES), 1)
    for g in range(G_A):
        qg = q_ref[0, :, g * LANES:(g + 1) * LANES].astype(F32)
        for c in range(2):
            keep = (lane >= DH_A) if c else (lane < DH_A)
            q4_ref[pl.ds((2 * g + c) * tq, tq), :] = jnp.where(keep, qg, 0.0).astype(BF16)
    mpart_ref[...] = jnp.full(mpart_ref.shape, -jnp.inf, F32)

    def score_pass(kb, near):
        rows = pl.ds(pl.multiple_of(kb * tq, tq), tq)
        s = _dot_nt(q4_ref[...], k_ref[0, rows, :])
        if near is not None:
            s = jnp.concatenate([s[i * tq:(i + 1) * tq] + bias_ref[kvh * G_A + i // 2, near]
                                 for i in range(4)], axis=0)
        _score_pass_tail(kb, s, s_ref, mpart_ref)

    def far_body(kb, carry):
        score_pass(kb, None)
        return carry

    lax.fori_loop(0, jnp.maximum(qi - 1, 0), far_body, 0)

    @pl.when(qi >= 1)
    def _():
        score_pass(qi - 1, 1)

    score_pass(qi, 0)
    mb_ref[...] = jnp.broadcast_to(jnp.max(mpart_ref[...], axis=-1, keepdims=True), mb_ref.shape)
    o_all = _value_pass(qi + 1, tq, s_ref, mb_ref, v_ref, lpart_ref, acc_ref)

    lam = _lam(lq1_ref, lk1_ref, lq2_ref, lk2_ref)
    for g in range(G_A):
        o = o_all[2 * g * tq:(2 * g + 1) * tq] - lam * o_all[(2 * g + 1) * tq:(2 * g + 2) * tq]
        y = _rms(o, sg_ref[...]) * (1.0 - LAM_INIT)
        o_ref[0, :, g * LANES:(g + 1) * LANES] = y.astype(BF16)


def _attn_scratch(tq, nq):
    m = 4 * tq
    return [pltpu.VMEM((m, LANES), BF16), pltpu.VMEM((nq, m, tq), F32), pltpu.VMEM((m, LANES), F32),
            pltpu.VMEM((m, LANES), F32), pltpu.VMEM((m, LANES), F32), pltpu.VMEM((m, LANES), F32)]


def _attn_a(rb_flat, qa, ka16, va16, lq1, lk1, lq2, lk2, sg, tq):
    b, t, _ = qa.shape
    assert t % tq == 0 and tq >= MAX_DISTANCE and tq % LANES == 0
    wq = G_A * 2 * DH_A
    qspec = pl.BlockSpec((1, tq, wq), lambda bi, k, i: (bi, i, k))
    kvspec = pl.BlockSpec((1, t, 2 * DH_A), lambda bi, k, i: (bi, 0, k))
    small = lambda a: pl.BlockSpec(a.shape, lambda bi, k, i: (0, 0))
    return pl.pallas_call(
        functools.partial(_attn_a_body, tq=tq), grid=(b, KV_A, t // tq),
        in_specs=[pl.BlockSpec(memory_space=pltpu.SMEM), qspec, kvspec, kvspec,
                  small(lq1), small(lk1), small(lq2), small(lk2), small(sg)],
        out_specs=qspec, out_shape=jax.ShapeDtypeStruct(qa.shape, BF16),
        scratch_shapes=[pltpu.VMEM((H_A, 2, tq, tq), F32), pltpu.VMEM((tq, tq), jnp.int32)]
        + _attn_scratch(tq, t // tq),
        compiler_params=_cparams("arbitrary", "arbitrary", "arbitrary"), name="attn_a",
    )(rb_flat, qa, ka16, va16, lq1, lk1, lq2, lk2, sg)


def _attn_b_body(q_ref, k_ref, v_ref, cq_ref, ck_ref, o_ref,
                 q4_ref, s_ref, mpart_ref, mb_ref, lpart_ref, acc_ref, *, tq):
    qi = pl.program_id(2)
    lane = lax.broadcasted_iota(jnp.int32, (tq, LANES), 1)
    heads = [2 * (i % 2) + i // 2 for i in range(4)]
    for s_ in range(2):
        qs = q_ref[0, :, s_ * LANES:(s_ + 1) * LANES].astype(F32)
        for half in range(2):
            keep = (lane >= DH_B) if half else (lane < DH_B)
            q4_ref[pl.ds((2 * s_ + half) * tq, tq), :] = jnp.where(keep, qs, 0.0).astype(BF16)
    mpart_ref[...] = jnp.full(mpart_ref.shape, -jnp.inf, F32)

    def score_pass(kb, diag):
        rows = pl.ds(pl.multiple_of(kb * tq, tq), tq)
        s = _dot_nt(q4_ref[...], k_ref[0, rows, :])
        s = jnp.concatenate([s[i * tq:(i + 1) * tq] - ck_ref[0, 0, kb, heads[i]:heads[i] + 1, :]
                             for i in range(4)], axis=0)
        if diag:
            r = lax.broadcasted_iota(jnp.int32, s.shape, 0) % tq
            c = lax.broadcasted_iota(jnp.int32, s.shape, 1)
            s = jnp.where(r >= c, s, NEG)
        _score_pass_tail(kb, s, s_ref, mpart_ref)

    def body(kb, carry):
        score_pass(kb, False)
        return carry

    lax.fori_loop(0, qi, body, 0)
    score_pass(qi, True)
    cq = jnp.concatenate([cq_ref[0, 0, :, heads[i]:heads[i] + 1] for i in range(4)], axis=0)
    m = jnp.max(mpart_ref[...], axis=-1, keepdims=True) + cq
    mb_ref[...] = jnp.broadcast_to(m - cq, mb_ref.shape)
    o_all = _value_pass(qi + 1, tq, s_ref, mb_ref, v_ref, lpart_ref, acc_ref)
    for s_ in range(2):
        o0 = o_all[2 * s_ * tq:(2 * s_ + 1) * tq]
        o1 = o_all[(2 * s_ + 1) * tq:(2 * s_ + 2) * tq]
        o_ref[0, :, s_ * LANES:(s_ + 1) * LANES] = jnp.where(lane < DH_B, o0, o1).astype(BF16)


def _attn_b(qb, kb16, vb16, cq4, ck4, tq):
    b, t, _ = qb.shape
    npair = KV_B // 2
    qspec = pl.BlockSpec((1, tq, 2 * LANES), lambda bi, k, i: (bi, i, k))
    kvspec = pl.BlockSpec((1, t, LANES), lambda bi, k, i: (bi, 0, k))
    cqspec = pl.BlockSpec((1, 1, tq, 4), lambda bi, k, i: (bi, k, i, 0))
    ckspec = pl.BlockSpec((1, 1, t // tq, 4, tq), lambda bi, k, i: (bi, k, 0, 0, 0))
    return pl.pallas_call(
        functools.partial(_attn_b_body, tq=tq), grid=(b, npair, t // tq),
        in_specs=[qspec, kvspec, kvspec, cqspec, ckspec],
        out_specs=qspec, out_shape=jax.ShapeDtypeStruct(qb.shape, BF16),
        scratch_shapes=_attn_scratch(tq, t // tq),
        compiler_params=_cparams("arbitrary", "arbitrary", "arbitrary"), name="attn_b",
    )(qb, kb16, vb16, cq4, ck4)


def _decode_body(pt_ref, wqa_ref, wqb_ref, kan_ref, van_ref, kbn_ref, vbn_ref, lfn_ref, rbt_ref, *rest,
                 page, pps):
    del pt_ref
    cak, cav, cbk, cbv, clf = (rest[i * pps:(i + 1) * pps] for i in range(5))
    (lq1_ref, lk1_ref, lq2_ref, lk2_ref, sg_ref, oa_ref, ob_ref,
     bias_ref, m_ref, l_ref, acc_ref, carry_ref) = rest[5 * pps:]
    p = pl.program_id(1)
    nrow = 2 * H_A

    @pl.when(p == 0)
    def _():
        sa = jnp.sum(wqa_ref[0].astype(F32) * kan_ref[0], axis=-1, keepdims=True) + rbt_ref[:, 0:1]
        sb = jnp.sum(wqb_ref[0].astype(F32) * kbn_ref[0], axis=-1, keepdims=True)
        m_ref[0] = jnp.broadcast_to(sa, m_ref.shape[1:])
        m_ref[1] = jnp.broadcast_to(sb, m_ref.shape[1:])
        l_ref[...] = jnp.ones(l_ref.shape, F32)
        acc_ref[0] = jnp.broadcast_to(van_ref[0], acc_ref.shape[1:])
        acc_ref[1] = jnp.broadcast_to(vbn_ref[0], acc_ref.shape[1:])
        carry_ref[...] = lfn_ref[0]
        dist = page - lax.broadcasted_iota(jnp.int32, (nrow, page), 1)
        bucket = _t5_bucket(dist)
        acc = jnp.zeros((nrow, page), F32)
        for i in range(N_BUCKETS):
            acc = jnp.where(bucket == i, rbt_ref[:, i:i + 1], acc)
        bias_ref[...] = jnp.broadcast_to(rbt_ref[:, N_BUCKETS - 1:N_BUCKETS], bias_ref.shape)
        bias_ref[:, 0:page] = acc

    @pl.when(p == 1)
    def _():
        bias_ref[...] = jnp.broadcast_to(rbt_ref[:, N_BUCKETS - 1:N_BUCKETS], bias_ref.shape)

    def update(idx, s, pv):
        m_prev = m_ref[idx]
        m_new = jnp.maximum(m_prev, jnp.max(s, axis=-1, keepdims=True))
        alpha = jnp.exp(m_prev - m_new)
        pr = jnp.exp(s - m_new[:, 0:1])
        l_ref[idx] = alpha * l_ref[idx] + jnp.sum(pr, axis=-1, keepdims=True)
        acc_ref[idx] = alpha[:, 0:1] * acc_ref[idx] + pv(pr.astype(BF16))
        m_ref[idx] = m_new

    kt_a = jnp.concatenate([r[0].astype(BF16) for r in cak], axis=1)
    s_a = _dot(wqa_ref[0], kt_a) + bias_ref[...]
    v_a = jnp.concatenate(
        [jnp.concatenate([r[0, pl.ds(kv, page, stride=KV_A), :] for kv in range(KV_A)], axis=1)
         for r in cav], axis=0).astype(BF16)
    update(0, s_a, lambda pr: _dot(pr, v_a))

    r = lax.broadcasted_iota(jnp.int32, (page, page), 0)
    c = lax.broadcasted_iota(jnp.int32, (page, page), 1)
    later = (r > c).astype(F32).astype(BF16)
    lfs = [ref[0] for ref in clf]
    suf = _dot(jnp.concatenate([piece for lf in lfs for piece in _split3(lf)], axis=0), later)
    carry = carry_ref[...]
    decay = []
    for j, lf in enumerate(lfs):
        o = 3 * H_B * j
        suffix = suf[o:o + H_B] + suf[o + H_B:o + 2 * H_B] + suf[o + 2 * H_B:o + 3 * H_B]
        decay.append(suffix + carry)
        carry = carry + suffix[:, 0:1] + lf[:, 0:1]
    carry_ref[...] = carry
    kt_b = jnp.concatenate([ref[0].astype(BF16) for ref in cbk], axis=1)
    s_b = _dot(wqb_ref[0], kt_b) + jnp.concatenate(decay, axis=1)
    vt_b = jnp.concatenate([ref[0].astype(BF16) for ref in cbv], axis=1)
    update(1, s_b, lambda pr: _dot_nt(pr, vt_b))

    @pl.when(p == pl.num_programs(1) - 1)
    def _():
        rowi = lax.broadcasted_iota(jnp.int32, acc_ref.shape[1:], 0)
        lanei = lax.broadcasted_iota(jnp.int32, acc_ref.shape[1:], 1)

        def own_block(o, keep):
            o = jnp.where(keep, o, 0.0)
            return functools.reduce(lambda a, b: a + b, _lane_chunks(o))

        oa = own_block(acc_ref[0] / l_ref[0][:, 0:1], (lanei // (2 * DH_A)) == ((rowi % H_A) // G_A))
        lam = _lam(lq1_ref, lk1_ref, lq2_ref, lk2_ref)
        o = oa[0:H_A] - lam * oa[H_A:2 * H_A]
        oa_ref[0] = _rms(o, sg_ref[...]) * (1.0 - LAM_INIT)
        ob_ref[0] = own_block(acc_ref[1] / l_ref[1][:, 0:1], (lanei // DH_B) == (rowi // G_B))


def _decode(page_table, wqa, wqb, ka_new, va_new, kb_new, vb_new, lf_new, rb_t,
            cak_t, cav2, cbk_t, cbv_t, clf_t, lq1, lk1, lq2, lk2, sg, pps):
    db, n_pages = page_table.shape
    page = cak_t.shape[2]
    width = cak_t.shape[1]
    assert page >= MAX_DISTANCE and page == LANES and n_pages % pps == 0
    per_b = lambda a: pl.BlockSpec((1,) + a.shape[1:], lambda b, p, pt: (b, 0, 0))
    small = lambda a: pl.BlockSpec(a.shape, lambda b, p, pt: (0, 0))

    def paged(a, j):
        return pl.BlockSpec((1,) + a.shape[1:],
                            lambda b, p, pt: (pt[b * n_pages + n_pages - 1 - (p * pps + j)], 0, 0))

    nrow = 2 * H_A
    caches = (cak_t, cav2, cbk_t, cbv_t, clf_t)
    grid_spec = pltpu.PrefetchScalarGridSpec(
        num_scalar_prefetch=1, grid=(db, n_pages // pps),
        in_specs=[per_b(wqa), per_b(wqb), per_b(ka_new), per_b(va_new), per_b(kb_new), per_b(vb_new),
                  per_b(lf_new), small(rb_t)]
        + [paged(a, j) for a in caches for j in range(pps)]
        + [small(lq1), small(lk1), small(lq2), small(lk2), small(sg)],
        out_specs=[pl.BlockSpec((1, H_A, 2 * DH_A), lambda b, p, pt: (b, 0, 0)),
                   pl.BlockSpec((1, H_B, LANES), lambda b, p, pt: (b, 0, 0))],
        scratch_shapes=[pltpu.VMEM((nrow, pps * page), F32), pltpu.VMEM((2, nrow, LANES), F32),
                        pltpu.VMEM((2, nrow, LANES), F32), pltpu.VMEM((2, nrow, width), F32),
                        pltpu.VMEM((H_B, 1), F32)])
    return pl.pallas_call(
        functools.partial(_decode_body, page=page, pps=pps), grid_spec=grid_spec,
        out_shape=[jax.ShapeDtypeStruct((db, H_A, 2 * DH_A), F32),
                   jax.ShapeDtypeStruct((db, H_B, LANES), F32)],
        compiler_params=_cparams("arbitrary", "arbitrary"), name="decode",
    )(page_table.reshape(-1), wqa, wqb, ka_new, va_new, kb_new, vb_new, lf_new, rb_t,
      *[a for a in caches for _ in range(pps)], lq1, lk1, lq2, lk2, sg)


def _merge_body(x_ref, oa_ref, ob_ref, ga_ref, gb_ref, wpa_ref, wpb_ref, wo_ref, x1_ref):
    pa = _dot(oa_ref[...], wpa_ref[...])
    pb = _dot(ob_ref[...], wpb_ref[...])
    h = ga_ref[...].astype(F32) * pa + gb_ref[...].astype(F32) * pb
    x1_ref[...] = x_ref[...] + _dot(h.astype(BF16), wo_ref[...])


def _merge(x2d, oa, ob, ga, gb, wpa, wpb, wo, tm):
    rows, d = x2d.shape
    row = pl.BlockSpec((tm, d), lambda i: (i, 0))
    const = lambda a: pl.BlockSpec(a.shape, lambda i: (0, 0))
    return pl.pallas_call(
        _merge_body, grid=(rows // tm,),
        in_specs=[row, row, row, row, row, const(wpa), const(wpb), const(wo)],
        out_specs=row, out_shape=jax.ShapeDtypeStruct(x2d.shape, F32),
        compiler_params=_cparams("arbitrary"), name="merge")(x2d, oa, ob, ga, gb, wpa, wpb, wo)


_FF_CHUNK = 1024


def _mlp_body(x_ref, g_ref, wu_ref, wd_ref, gf_ref, y_ref):
    x = x_ref[...]
    hn = _rms(x, g_ref[...]).astype(BF16)
    acc = x
    for c in range(D_FF // _FF_CHUNK):
        sl = slice(c * _FF_CHUNK, (c + 1) * _FF_CHUNK)
        u = jnp.maximum(_dot(hn, wu_ref[:, sl]), 0.0)
        acc = acc + _dot((u * u).astype(BF16), wd_ref[sl, :])
    y_ref[...] = _rms(acc, gf_ref[...])


def _mlp(x2d, g, wu, wd, gf, tm):
    rows, d = x2d.shape
    row = pl.BlockSpec((tm, d), lambda i: (i, 0))
    const = lambda a: pl.BlockSpec(a.shape, lambda i: (0, 0))
    return pl.pallas_call(
        _mlp_body, grid=(rows // tm,),
        in_specs=[row, const(g), const(wu), const(wd), const(gf)],
        out_specs=row, out_shape=jax.ShapeDtypeStruct(x2d.shape, F32),
        compiler_params=_cparams("arbitrary"), name="mlp")(x2d, g, wu, wd, gf)


def _row_tile(rows, want):
    return want if rows % want == 0 else rows


def kernel(x_prompt, x_sample, cache_a_k, cache_a_v, cache_b_k, cache_b_v, cache_b_logf, page_table,
           norm_attn_g, w_in, b_forget, rel_bias, lam_q1, lam_k1, lam_q2, lam_k2, subln_g,
           w_proj_a, w_proj_b, w_out, norm_mlp_g, w_up, w_down, norm_final_g):
    depth = w_in.shape[0]
    assert depth == 1, "one layer supported"
    B, T, D = x_prompt.shape
    DB, TS, _ = x_sample.shape
    assert TS == 1 and D == D_MODEL
    n_pool, page = cache_a_k.shape[1], cache_a_k.shape[2]

    w = w_in[0]
    splits = np.cumsum([W_QA, W_KA, W_VA, W_QB, W_KB, W_VB, H_B, D_MODEL])
    w_qa, w_ka, w_va, w_qb, w_kb, w_vb, w_f, w_ga, w_gb = jnp.split(w, [int(s) for s in splits], axis=1)
    w_qb = w_qb.reshape(D, H_B, DH_B)[:, _B_HEAD_PERM].reshape(D, W_QB)
    w_main = jnp.concatenate([w_qa, w_ka, w_va, w_qb, w_kb, w_vb, w_ga, w_gb], axis=1).astype(BF16)
    w_fl = jnp.pad(w_f, ((0, 0), (0, LANES - H_B))).astype(BF16)
    b_fl = jnp.pad(b_forget[0].astype(F32), (0, LANES - H_B)).reshape(1, LANES)
    g_attn = norm_attn_g[0].reshape(1, D).astype(F32)
    wpa = w_proj_a[0].astype(BF16)
    wpb = w_proj_b[0].reshape(H_B, DH_B, D)[_B_HEAD_PERM].reshape(H_B * DH_B, D).astype(BF16)
    wo = w_out[0].astype(BF16)
    wu = w_up[0].astype(BF16)
    wd = w_down[0].astype(BF16)
    g_mlp = norm_mlp_g[0].reshape(1, D).astype(F32)
    g_fin = norm_final_g.reshape(1, D).astype(F32)
    lq1, lk1, lq2, lk2 = (a[0].reshape(1, DH_A).astype(F32) for a in (lam_q1, lam_k1, lam_q2, lam_k2))
    sg = subln_g[0].reshape(1, 2 * DH_A).astype(F32)
    rb = rel_bias.astype(F32)

    xp = x_prompt.reshape(B * T, D)
    tm = _row_tile(B * T, 256)
    (qa, qb, ga, gb, ka, va, kb, vb, lf, ka16, va16, kb16, vb16) = _in_proj(xp, g_attn, w_main, w_fl, b_fl, tm)
    c_t = _cumsum_t(lf.reshape(B, T, H_B).transpose(0, 2, 1))
    tq = _row_tile(T, 256)
    c4 = c_t.reshape(B, H_B // 4, 4, T)
    cq4 = c4.transpose(0, 1, 3, 2)
    ck4 = c4.reshape(B, H_B // 4, 4, T // tq, tq).transpose(0, 1, 3, 2, 4)
    oa = _attn_a(rb.reshape(-1), qa.reshape(B, T, W_QA), ka16.reshape(B, T, W_KA), va16.reshape(B, T, W_VA),
                 lq1, lk1, lq2, lk2, sg, tq)
    ob = _attn_b(qb.reshape(B, T, W_QB), kb16.reshape(B, T, W_KB), vb16.reshape(B, T, W_VB), cq4, ck4, tq)
    x1 = _merge(xp, oa.reshape(B * T, W_QA), ob.reshape(B * T, W_QB), ga, gb, wpa, wpb, wo, tm)
    y_prompt = _mlp(x1, g_mlp, wu, wd, g_fin, tm).reshape(B, T, D)

    xs = x_sample.reshape(DB, D)
    tms = _row_tile(DB, 128)
    (qa_s, qb_s, ga_s, gb_s, ka_s, va_s, kb_s, vb_s, lf_s, _, _, _, _) = _in_proj(xs, g_attn, w_main, w_fl, b_fl, tms)
    qa5 = qa_s.reshape(DB, KV_A, G_A, 2, DH_A)
    wqa = jnp.einsum("bkgcd,kK,cC->bckgKCd", qa5, jnp.eye(KV_A, dtype=BF16), jnp.eye(2, dtype=BF16))
    wqa = wqa.reshape(DB, 2 * H_A, W_KA)
    qb4 = qb_s.reshape(DB, H_B, DH_B)[:, _B_HEAD_INV].reshape(DB, KV_B, G_B, DH_B)
    wqb = jnp.einsum("bkgd,kK->bkgKd", qb4, jnp.eye(KV_B, dtype=BF16)).reshape(DB, H_B, W_KB)
    pps = math.gcd(page_table.shape[1], 8)
    rb_t = jnp.tile(rb.T, (2, 1))
    oa_s, ob_raw = _decode(
        page_table, wqa, wqb, ka_s.reshape(DB, 1, W_KA), va_s.reshape(DB, 1, W_VA),
        kb_s.reshape(DB, 1, W_KB), vb_s.reshape(DB, 1, W_VB), lf_s.reshape(DB, H_B, 1), rb_t,
        cache_a_k[0].transpose(0, 2, 3, 4, 1).reshape(n_pool, W_KA, page),
        cache_a_v[0].reshape(n_pool, page * KV_A, 2 * DH_A),
        cache_b_k[0].transpose(0, 2, 3, 1).reshape(n_pool, W_KB, page),
        cache_b_v[0].transpose(0, 2, 3, 1).reshape(n_pool, W_VB, page),
        cache_b_logf[0].transpose(0, 2, 1), lq1, lk1, lq2, lk2, sg, pps)
    odd_kv = ((jnp.arange(H_B) // G_B) % 2 == 1)[None, :, None]
    ob_s = jnp.where(odd_kv, ob_raw[..., DH_B:], ob_raw[..., :DH_B])
    ob_s = ob_s[:, _B_HEAD_PERM].reshape(DB, W_QB).astype(BF16)
    x1_s = _merge(xs, oa_s.reshape(DB, W_QA).astype(BF16), ob_s, ga_s, gb_s, wpa, wpb, wo, tms)
    y_sample = _mlp(x1_s, g_mlp, wu, wd, g_fin, tms).reshape(DB, 1, D)

    return (y_prompt, y_sample,
            ka.reshape(1, B, T, KV_A, 2, DH_A), va.reshape(1, B, T, KV_A, 2 * DH_A),
            kb.reshape(1, B, T, KV_B, DH_B), vb.reshape(1, B, T, KV_B, DH_B), lf.reshape(1, B, T, H_B),
            ka_s.reshape(1, DB, 1, KV_A, 2, DH_A), va_s.reshape(1, DB, 1, KV_A, 2 * DH_A),
            kb_s.reshape(1, DB, 1, KV_B, DH_B), vb_s.reshape(1, DB, 1, KV_B, DH_B),
            lf_s.reshape(1, DB, 1, H_B))
```

```python
import functools
import math

import jax
import jax.numpy as jnp
import numpy as np
from jax import lax
from jax.experimental import pallas as pl
from jax.experimental.pallas import tpu as pltpu

F32 = jnp.float32
BF16 = jnp.bfloat16

D_MODEL = 1024
H_A, KV_A, DH_A = 8, 4, 64
G_A = H_A // KV_A
H_B, KV_B, DH_B = 16, 8, 64
G_B = H_B // KV_B
D_FF = 4 * D_MODEL
N_BUCKETS = 32
MAX_DISTANCE = 128
EPS = 1e-6
NEG = -1e30
LAM_INIT = 0.8 - 0.6 * math.exp(-0.3 * 0)
LOG2E = math.log2(math.e)
Q_SCALE = DH_A ** -0.5 * LOG2E
assert DH_A == DH_B and G_A == 2 and G_B == 2

LANES = 128
VMEM_LIMIT_BYTES = 56 * 1024 * 1024

W_QA, W_KA, W_VA = H_A * 2 * DH_A, KV_A * 2 * DH_A, KV_A * 2 * DH_A
W_QB, W_KB, W_VB = H_B * DH_B, KV_B * DH_B, KV_B * DH_B
_OFFS = np.cumsum([0, W_QA, W_KA, W_VA, W_QB, W_KB, W_VB, D_MODEL, D_MODEL])
N_MAIN = int(_OFFS[-1])

_B_HEAD_PERM = np.array([4 * i + j for i in range(H_B // 4) for j in (0, 2, 1, 3)])
_B_HEAD_INV = np.argsort(_B_HEAD_PERM)


def _cparams(*sem):
    return pltpu.CompilerParams(dimension_semantics=sem, vmem_limit_bytes=VMEM_LIMIT_BYTES)


def _rms(x, g):
    return (x * lax.rsqrt(jnp.mean(x * x, axis=-1, keepdims=True) + EPS)) * g


def _t5_bucket(dist):
    n = jnp.maximum(dist, 0)
    max_exact = N_BUCKETS // 2
    nf = jnp.maximum(n, max_exact).astype(F32)
    large = max_exact + (jnp.log(nf / max_exact) / math.log(MAX_DISTANCE / max_exact)
                         * (N_BUCKETS - max_exact)).astype(jnp.int32)
    large = jnp.minimum(large, N_BUCKETS - 1)
    return jnp.where(n < max_exact, n, large)


def _split3(x):
    hi = x.astype(BF16)
    r1 = x - hi.astype(F32)
    mid = r1.astype(BF16)
    lo = (r1 - mid.astype(F32)).astype(BF16)
    return hi, mid, lo


def _dot(a, b):
    return jnp.dot(a, b, preferred_element_type=F32)


def _dot_nt(a, b):
    return lax.dot_general(a, b, (((1,), (1,)), ((), ())), preferred_element_type=F32)


def _in_proj_body(x_ref, g_ref, w_ref, wfl_ref, bfl_ref,
                  qa_ref, qb_ref, ga_ref, gb_ref, kat_ref, va4_ref, kbt_ref, vbt_ref, lft_ref,
                  ka16_ref, va16_ref, kb16_ref, vb16_ref):
    xb = _rms(x_ref[...], g_ref[...]).astype(BF16)
    tm = xb.shape[0]

    def seg(i):
        return _dot(xb, w_ref[:, int(_OFFS[i]):int(_OFFS[i + 1])])

    qa_ref[...] = (seg(0) * Q_SCALE).astype(BF16)
    for i, ot, o16 in ((1, kat_ref, ka16_ref), (4, kbt_ref, kb16_ref), (5, vbt_ref, vb16_ref)):
        z = seg(i)
        ot[0] = z.T
        o16[...] = z.astype(BF16)
    z = seg(2)
    va16_ref[...] = z.astype(BF16)
    for kv in range(KV_A):
        va4_ref[0, pl.ds(kv, tm, stride=KV_A), :] = z[:, kv * 2 * DH_A:(kv + 1) * 2 * DH_A]
    qb_ref[...] = (seg(3) * Q_SCALE).astype(BF16)
    ga_ref[...] = (1.0 / (1.0 + jnp.exp(-seg(6)))).astype(BF16)
    gb_ref[...] = (1.0 / (1.0 + jnp.exp(-seg(7)))).astype(BF16)
    zf = _dot(xb, wfl_ref[...]).T[:H_B] + bfl_ref[...]
    lft_ref[0] = jnp.minimum(zf, 0.0) - jnp.log1p(jnp.exp(-jnp.abs(zf)))


def _in_proj(x2d, g, w_main, w_fl, b_fl, tm, seq):
    rows, d = x2d.shape
    nb, nt = rows // seq, seq // tm
    row = lambda n: pl.BlockSpec((tm, n), lambda i: (i, 0))
    const = lambda a: pl.BlockSpec(a.shape, lambda i: (0, 0))
    tspec = lambda n: pl.BlockSpec((1, n, tm), lambda i: (i // nt, 0, i % nt))
    widths16 = (W_QA, W_QB, D_MODEL, D_MODEL)
    widths32 = (W_KA, W_VA, W_KB, W_VB)
    out_shape = ([jax.ShapeDtypeStruct((rows, n), BF16) for n in widths16]
                 + [jax.ShapeDtypeStruct((nb, W_KA, seq), F32),
                    jax.ShapeDtypeStruct((nb, seq * KV_A, 2 * DH_A), F32),
                    jax.ShapeDtypeStruct((nb, W_KB, seq), F32),
                    jax.ShapeDtypeStruct((nb, W_VB, seq), F32),
                    jax.ShapeDtypeStruct((nb, H_B, seq), F32)]
                 + [jax.ShapeDtypeStruct((rows, n), BF16) for n in widths32])
    out_specs = ([row(n) for n in widths16]
                 + [tspec(W_KA), pl.BlockSpec((1, tm * KV_A, 2 * DH_A), lambda i: (i // nt, i % nt, 0)),
                    tspec(W_KB), tspec(W_VB), tspec(H_B)]
                 + [row(n) for n in widths32])
    return pl.pallas_call(
        _in_proj_body, grid=(rows // tm,),
        in_specs=[row(d), const(g), const(w_main), const(w_fl), const(b_fl)],
        out_specs=out_specs, out_shape=out_shape,
        compiler_params=_cparams("arbitrary"), name="in_proj")(x2d, g, w_main, w_fl, b_fl)


_CS_BLK = 256


def _cumsum_body(lf_ref, c_ref):
    t = lf_ref.shape[2]
    r = lax.broadcasted_iota(jnp.int32, (_CS_BLK, _CS_BLK), 0)
    c = lax.broadcasted_iota(jnp.int32, (_CS_BLK, _CS_BLK), 1)
    tri = (r <= c).astype(F32).astype(BF16)
    carry = jnp.zeros((H_B, 1), F32)
    for i in range(t // _CS_BLK):
        sl = slice(i * _CS_BLK, (i + 1) * _CS_BLK)
        hi, mid, lo = _split3(lf_ref[0, :, sl])
        cs = _dot(hi, tri) + _dot(mid, tri) + _dot(lo, tri) + carry
        c_ref[0, :, sl] = cs
        carry = cs[:, _CS_BLK - 1:_CS_BLK]


def _cumsum_t(lf_t):
    b, h, t = lf_t.shape
    assert t % _CS_BLK == 0
    spec = pl.BlockSpec((1, h, t), lambda i: (i, 0, 0))
    return pl.pallas_call(
        _cumsum_body, grid=(b,), in_specs=[spec], out_specs=spec,
        out_shape=jax.ShapeDtypeStruct(lf_t.shape, F32),
        compiler_params=_cparams("arbitrary"), name="cumsum")(lf_t)


def _lam(lq1_ref, lk1_ref, lq2_ref, lk2_ref):
    return (jnp.exp(jnp.sum(lq1_ref[...] * lk1_ref[...], keepdims=True))
            - jnp.exp(jnp.sum(lq2_ref[...] * lk2_ref[...], keepdims=True)) + LAM_INIT)


def _lane_chunks(x):
    return [x[:, j * LANES:(j + 1) * LANES] for j in range(x.shape[1] // LANES)]


def _keep_scores(kb, nb, tq, s, s_ref, mpart_ref):
    for j in range(nb):
        s_ref[kb + j] = s[:, j * tq:(j + 1) * tq]
    mp = mpart_ref[...]
    for sj in _lane_chunks(s):
        mp = jnp.maximum(mp, sj)
    mpart_ref[...] = mp


def _paired(n, fn):
    def body(i, carry):
        fn(2 * i, 2)
        return carry

    lax.fori_loop(0, n // 2, body, 0)

    @pl.when(n % 2 == 1)
    def _():
        fn(n - 1, 1)


def _value_pass(n_blocks, tq, s_ref, mb_ref, v_ref, lpart_ref, acc_ref):
    lpart_ref[...] = jnp.zeros(lpart_ref.shape, F32)
    acc_ref[...] = jnp.zeros(acc_ref.shape, F32)

    def blocks(kb, nb):
        mb = mb_ref[...]
        ps = [jnp.exp2(sj - mb) for j in range(nb) for sj in _lane_chunks(s_ref[kb + j])]
        lpart_ref[...] += functools.reduce(lambda a, b: a + b, ps)
        p16 = jnp.concatenate(ps, axis=1).astype(BF16)
        rows = pl.ds(pl.multiple_of(kb * tq, tq), nb * tq)
        acc_ref[...] += _dot(p16, v_ref[0, rows, :])

    _paired(n_blocks, blocks)
    return acc_ref[...] / jnp.sum(lpart_ref[...], axis=-1, keepdims=True)


def _attn_a_body(rb_ref, q_ref, k_ref, v_ref, lq1_ref, lk1_ref, lq2_ref, lk2_ref, sg_ref,
                 o_ref, bias_ref, bucket_ref, q4_ref, s_ref, mpart_ref, mb_ref, lpart_ref, acc_ref, *, tq):
    b, kvh, qi = pl.program_id(0), pl.program_id(1), pl.program_id(2)
    far = (N_BUCKETS - 1) * H_A

    @pl.when((b == 0) & (kvh == 0) & (qi == 0))
    def _():
        r = lax.broadcasted_iota(jnp.int32, (tq, tq), 0)
        c = lax.broadcasted_iota(jnp.int32, (tq, tq), 1)
        for t in range(2):
            dist = r - c + t * tq
            bucket_ref[...] = _t5_bucket(dist)
            for h in range(H_A):
                bias_ref[h, t] = jnp.zeros((tq, tq), F32)

                def body(i, carry, h=h, t=t):
                    rel = (rb_ref[i * H_A + h] - rb_ref[far + h]) * LOG2E
                    bias_ref[h, t] = jnp.where(bucket_ref[...] == i, rel, bias_ref[h, t])
                    return carry

                lax.fori_loop(0, N_BUCKETS, body, 0)
                if t == 0:
                    bias_ref[h, t] = jnp.where(dist >= 0, bias_ref[h, t], NEG)

    lane = lax.broadcasted_iota(jnp.int32, (tq, LANES), 1)
    for g in range(G_A):
        qg = q_ref[0, :, g * LANES:(g + 1) * LANES].astype(F32)
        for c in range(2):
            keep = (lane >= DH_A) if c else (lane < DH_A)
            q4_ref[pl.ds((2 * g + c) * tq, tq), :] = jnp.where(keep, qg, 0.0).astype(BF16)
    mpart_ref[...] = jnp.full(mpart_ref.shape, -jnp.inf, F32)

    def score_pass(kb, nb, near=None):
        rows = pl.ds(pl.multiple_of(kb * tq, tq), nb * tq)
        s = _dot_nt(q4_ref[...], k_ref[0, rows, :])
        if near is not None:
            s = jnp.concatenate([s[i * tq:(i + 1) * tq] + bias_ref[kvh * G_A + i // 2, near]
                                 for i in range(4)], axis=0)
        _keep_scores(kb, nb, tq, s, s_ref, mpart_ref)

    _paired(jnp.maximum(qi - 1, 0), score_pass)

    @pl.when(qi >= 1)
    def _():
        score_pass(qi - 1, 1, 1)

    score_pass(qi, 1, 0)
    mb_ref[...] = jnp.broadcast_to(jnp.max(mpart_ref[...], axis=-1, keepdims=True), mb_ref.shape)
    o_all = _value_pass(qi + 1, tq, s_ref, mb_ref, v_ref, lpart_ref, acc_ref)

    lam = _lam(lq1_ref, lk1_ref, lq2_ref, lk2_ref)
    for g in range(G_A):
        o = o_all[2 * g * tq:(2 * g + 1) * tq] - lam * o_all[(2 * g + 1) * tq:(2 * g + 2) * tq]
        y = _rms(o, sg_ref[...]) * (1.0 - LAM_INIT)
        o_ref[0, :, g * LANES:(g + 1) * LANES] = y.astype(BF16)


def _attn_scratch(tq, nq):
    m = 4 * tq
    return [pltpu.VMEM((m, LANES), BF16), pltpu.VMEM((nq, m, tq), F32), pltpu.VMEM((m, LANES), F32),
            pltpu.VMEM((m, LANES), F32), pltpu.VMEM((m, LANES), F32), pltpu.VMEM((m, LANES), F32)]


def _attn_a(rb_flat, qa, ka16, va16, lq1, lk1, lq2, lk2, sg, tq):
    b, t, _ = qa.shape
    assert t % tq == 0 and tq >= MAX_DISTANCE and tq % LANES == 0
    wq = G_A * 2 * DH_A
    qspec = pl.BlockSpec((1, tq, wq), lambda bi, k, i: (bi, i, k))
    kvspec = pl.BlockSpec((1, t, 2 * DH_A), lambda bi, k, i: (bi, 0, k))
    small = lambda a: pl.BlockSpec(a.shape, lambda bi, k, i: (0, 0))
    return pl.pallas_call(
        functools.partial(_attn_a_body, tq=tq), grid=(b, KV_A, t // tq),
        in_specs=[pl.BlockSpec(memory_space=pltpu.SMEM), qspec, kvspec, kvspec,
                  small(lq1), small(lk1), small(lq2), small(lk2), small(sg)],
        out_specs=qspec, out_shape=jax.ShapeDtypeStruct(qa.shape, BF16),
        scratch_shapes=[pltpu.VMEM((H_A, 2, tq, tq), F32), pltpu.VMEM((tq, tq), jnp.int32)]
        + _attn_scratch(tq, t // tq),
        compiler_params=_cparams("arbitrary", "arbitrary", "arbitrary"), name="attn_a",
    )(rb_flat, qa, ka16, va16, lq1, lk1, lq2, lk2, sg)


def _attn_b_body(q_ref, k_ref, v_ref, cq_ref, ck_ref, o_ref,
                 q4_ref, s_ref, mpart_ref, mb_ref, lpart_ref, acc_ref, *, tq):
    qi = pl.program_id(2)
    lane = lax.broadcasted_iota(jnp.int32, (tq, LANES), 1)
    heads = [2 * (i % 2) + i // 2 for i in range(4)]
    for s_ in range(2):
        qs = q_ref[0, :, s_ * LANES:(s_ + 1) * LANES].astype(F32)
        for half in range(2):
            keep = (lane >= DH_B) if half else (lane < DH_B)
            q4_ref[pl.ds((2 * s_ + half) * tq, tq), :] = jnp.where(keep, qs, 0.0).astype(BF16)
    mpart_ref[...] = jnp.full(mpart_ref.shape, -jnp.inf, F32)

    r = lax.broadcasted_iota(jnp.int32, (tq, tq), 0)
    c = lax.broadcasted_iota(jnp.int32, (tq, tq), 1)
    causal = r >= c

    def score_pass(kb, nb, diag=False):
        rows = pl.ds(pl.multiple_of(kb * tq, tq), nb * tq)
        s = _dot_nt(q4_ref[...], k_ref[0, rows, :])
        parts = []
        for i in range(4):
            ck = jnp.concatenate([ck_ref[0, 0, kb + j, heads[i]:heads[i] + 1, :] for j in range(nb)], axis=1)
            si = s[i * tq:(i + 1) * tq] - ck * LOG2E
            parts.append(jnp.where(causal, si, NEG) if diag else si)
        _keep_scores(kb, nb, tq, jnp.concatenate(parts, axis=0), s_ref, mpart_ref)

    _paired(qi, score_pass)
    score_pass(qi, 1, True)
    cq = jnp.concatenate([cq_ref[0, 0, :, heads[i]:heads[i] + 1] for i in range(4)], axis=0) * LOG2E
    m = jnp.max(mpart_ref[...], axis=-1, keepdims=True) + cq
    mb_ref[...] = jnp.broadcast_to(m - cq, mb_ref.shape)
    o_all = _value_pass(qi + 1, tq, s_ref, mb_ref, v_ref, lpart_ref, acc_ref)
    for s_ in range(2):
        o0 = o_all[2 * s_ * tq:(2 * s_ + 1) * tq]
        o1 = o_all[(2 * s_ + 1) * tq:(2 * s_ + 2) * tq]
        o_ref[0, :, s_ * LANES:(s_ + 1) * LANES] = jnp.where(lane < DH_B, o0, o1).astype(BF16)


def _attn_b(qb, kb16, vb16, cq4, ck4, tq):
    b, t, _ = qb.shape
    npair = KV_B // 2
    qspec = pl.BlockSpec((1, tq, 2 * LANES), lambda bi, k, i: (bi, i, k))
    kvspec = pl.BlockSpec((1, t, LANES), lambda bi, k, i: (bi, 0, k))
    cqspec = pl.BlockSpec((1, 1, tq, 4), lambda bi, k, i: (bi, k, i, 0))
    ckspec = pl.BlockSpec((1, 1, t // tq, 4, tq), lambda bi, k, i: (bi, k, 0, 0, 0))
    return pl.pallas_call(
        functools.partial(_attn_b_body, tq=tq), grid=(b, npair, t // tq),
        in_specs=[qspec, kvspec, kvspec, cqspec, ckspec],
        out_specs=qspec, out_shape=jax.ShapeDtypeStruct(qb.shape, BF16),
        scratch_shapes=_attn_scratch(tq, t // tq),
        compiler_params=_cparams("arbitrary", "arbitrary", "arbitrary"), name="attn_b",
    )(qb, kb16, vb16, cq4, ck4)


def _decode_body(pt_ref, wqa_ref, wqb_ref, kan_ref, van_ref, kbn_ref, vbn_ref, lfn_ref, rbt_ref, *rest,
                 page, pps):
    del pt_ref
    cak, cav, cbk, cbv, clf = (rest[i * pps:(i + 1) * pps] for i in range(5))
    (lq1_ref, lk1_ref, lq2_ref, lk2_ref, sg_ref, oa_ref, ob_ref,
     bias_ref, m_ref, l_ref, acc_ref, carry_ref) = rest[5 * pps:]
    p = pl.program_id(1)
    nrow = 2 * H_A

    @pl.when(p == 0)
    def _():
        sa = jnp.sum(wqa_ref[0].astype(F32) * kan_ref[0], axis=-1, keepdims=True) + rbt_ref[:, 0:1] * LOG2E
        sb = jnp.sum(wqb_ref[0].astype(F32) * kbn_ref[0], axis=-1, keepdims=True)
        m_ref[0] = jnp.broadcast_to(sa, m_ref.shape[1:])
        m_ref[1] = jnp.broadcast_to(sb, m_ref.shape[1:])
        l_ref[...] = jnp.ones(l_ref.shape, F32)
        acc_ref[0] = jnp.broadcast_to(van_ref[0], acc_ref.shape[1:])
        acc_ref[1] = jnp.broadcast_to(vbn_ref[0], acc_ref.shape[1:])
        carry_ref[...] = lfn_ref[0]
        dist = page - lax.broadcasted_iota(jnp.int32, (nrow, page), 1)
        bucket = _t5_bucket(dist)
        acc = jnp.zeros((nrow, page), F32)
        for i in range(N_BUCKETS):
            acc = jnp.where(bucket == i, rbt_ref[:, i:i + 1], acc)
        bias_ref[...] = jnp.broadcast_to(rbt_ref[:, N_BUCKETS - 1:N_BUCKETS] * LOG2E, bias_ref.shape)
        bias_ref[:, 0:page] = acc * LOG2E

    @pl.when(p == 1)
    def _():
        bias_ref[...] = jnp.broadcast_to(rbt_ref[:, N_BUCKETS - 1:N_BUCKETS] * LOG2E, bias_ref.shape)

    def update(idx, s, pv):
        m_prev = m_ref[idx]
        m_new = jnp.maximum(m_prev, jnp.max(s, axis=-1, keepdims=True))
        alpha = jnp.exp2(m_prev - m_new)
        pr = jnp.exp2(s - m_new[:, 0:1])
        l_ref[idx] = alpha * l_ref[idx] + jnp.sum(pr, axis=-1, keepdims=True)
        acc_ref[idx] = alpha[:, 0:1] * acc_ref[idx] + pv(pr.astype(BF16))
        m_ref[idx] = m_new

    kt_a = jnp.concatenate([r[0].astype(BF16) for r in cak], axis=1)
    s_a = _dot(wqa_ref[0], kt_a) + bias_ref[...]
    v_a = jnp.concatenate(
        [jnp.concatenate([r[0, pl.ds(kv, page, stride=KV_A), :] for kv in range(KV_A)], axis=1)
         for r in cav], axis=0).astype(BF16)
    update(0, s_a, lambda pr: _dot(pr, v_a))

    r = lax.broadcasted_iota(jnp.int32, (page, page), 0)
    c = lax.broadcasted_iota(jnp.int32, (page, page), 1)
    later = (r > c).astype(F32).astype(BF16)
    lfs = [ref[0] for ref in clf]
    suf = _dot(jnp.concatenate([piece for lf in lfs for piece in _split3(lf)], axis=0), later)
    carry = carry_ref[...]
    decay = []
    for j, lf in enumerate(lfs):
        o = 3 * H_B * j
        suffix = suf[o:o + H_B] + suf[o + H_B:o + 2 * H_B] + suf[o + 2 * H_B:o + 3 * H_B]
        decay.append(suffix + carry)
        carry = carry + suffix[:, 0:1] + lf[:, 0:1]
    carry_ref[...] = carry
    kt_b = jnp.concatenate([ref[0].astype(BF16) for ref in cbk], axis=1)
    s_b = _dot(wqb_ref[0], kt_b) + jnp.concatenate(decay, axis=1) * LOG2E
    vt_b = jnp.concatenate([ref[0].astype(BF16) for ref in cbv], axis=1)
    update(1, s_b, lambda pr: _dot_nt(pr, vt_b))

    @pl.when(p == pl.num_programs(1) - 1)
    def _():
        rowi = lax.broadcasted_iota(jnp.int32, acc_ref.shape[1:], 0)
        lanei = lax.broadcasted_iota(jnp.int32, acc_ref.shape[1:], 1)

        def own_block(o, keep):
            o = jnp.where(keep, o, 0.0)
            return functools.reduce(lambda a, b: a + b, _lane_chunks(o))

        oa = own_block(acc_ref[0] / l_ref[0][:, 0:1], (lanei // (2 * DH_A)) == ((rowi % H_A) // G_A))
        lam = _lam(lq1_ref, lk1_ref, lq2_ref, lk2_ref)
        o = oa[0:H_A] - lam * oa[H_A:2 * H_A]
        oa_ref[0] = _rms(o, sg_ref[...]) * (1.0 - LAM_INIT)
        ob_ref[0] = own_block(acc_ref[1] / l_ref[1][:, 0:1], (lanei // DH_B) == (rowi // G_B))


def _decode(page_table, wqa, wqb, ka_new, va_new, kb_new, vb_new, lf_new, rb_t,
            cak_t, cav2, cbk_t, cbv_t, clf_t, lq1, lk1, lq2, lk2, sg, pps):
    db, n_pages = page_table.shape
    page = cak_t.shape[2]
    width = cak_t.shape[1]
    assert page >= MAX_DISTANCE and page == LANES and n_pages % pps == 0
    per_b = lambda a: pl.BlockSpec((1,) + a.shape[1:], lambda b, p, pt: (b, 0, 0))
    small = lambda a: pl.BlockSpec(a.shape, lambda b, p, pt: (0, 0))

    def paged(a, j):
        return pl.BlockSpec((1,) + a.shape[1:],
                            lambda b, p, pt: (pt[b * n_pages + n_pages - 1 - (p * pps + j)], 0, 0))

    nrow = 2 * H_A
    caches = (cak_t, cav2, cbk_t, cbv_t, clf_t)
    grid_spec = pltpu.PrefetchScalarGridSpec(
        num_scalar_prefetch=1, grid=(db, n_pages // pps),
        in_specs=[per_b(wqa), per_b(wqb), per_b(ka_new), per_b(va_new), per_b(kb_new), per_b(vb_new),
                  per_b(lf_new), small(rb_t)]
        + [paged(a, j) for a in caches for j in range(pps)]
        + [small(lq1), small(lk1), small(lq2), small(lk2), small(sg)],
        out_specs=[pl.BlockSpec((1, H_A, 2 * DH_A), lambda b, p, pt: (b, 0, 0)),
                   pl.BlockSpec((1, H_B, LANES), lambda b, p, pt: (b, 0, 0))],
        scratch_shapes=[pltpu.VMEM((nrow, pps * page), F32), pltpu.VMEM((2, nrow, LANES), F32),
                        pltpu.VMEM((2, nrow, LANES), F32), pltpu.VMEM((2, nrow, width), F32),
                        pltpu.VMEM((H_B, 1), F32)])
    return pl.pallas_call(
        functools.partial(_decode_body, page=page, pps=pps), grid_spec=grid_spec,
        out_shape=[jax.ShapeDtypeStruct((db, H_A, 2 * DH_A), F32),
                   jax.ShapeDtypeStruct((db, H_B, LANES), F32)],
        compiler_params=_cparams("arbitrary", "arbitrary"), name="decode",
    )(page_table.reshape(-1), wqa, wqb, ka_new, va_new, kb_new, vb_new, lf_new, rb_t,
      *[a for a in caches for _ in range(pps)], lq1, lk1, lq2, lk2, sg)


def _merge_body(x_ref, oa_ref, ob_ref, ga_ref, gb_ref, wpa_ref, wpb_ref, wo_ref, x1_ref):
    pa = _dot(oa_ref[...], wpa_ref[...])
    pb = _dot(ob_ref[...], wpb_ref[...])
    h = ga_ref[...].astype(F32) * pa + gb_ref[...].astype(F32) * pb
    x1_ref[...] = x_ref[...] + _dot(h.astype(BF16), wo_ref[...])


def _merge(x2d, oa, ob, ga, gb, wpa, wpb, wo, tm):
    rows, d = x2d.shape
    row = pl.BlockSpec((tm, d), lambda i: (i, 0))
    const = lambda a: pl.BlockSpec(a.shape, lambda i: (0, 0))
    return pl.pallas_call(
        _merge_body, grid=(rows // tm,),
        in_specs=[row, row, row, row, row, const(wpa), const(wpb), const(wo)],
        out_specs=row, out_shape=jax.ShapeDtypeStruct(x2d.shape, F32),
        compiler_params=_cparams("arbitrary"), name="merge")(x2d, oa, ob, ga, gb, wpa, wpb, wo)


_FF_CHUNK = 1024


def _mlp_body(x_ref, g_ref, wu_ref, wd_ref, gf_ref, y_ref):
    x = x_ref[...]
    hn = _rms(x, g_ref[...]).astype(BF16)
    acc = x
    for c in range(D_FF // _FF_CHUNK):
        sl = slice(c * _FF_CHUNK, (c + 1) * _FF_CHUNK)
        u = jnp.maximum(_dot(hn, wu_ref[:, sl]), 0.0)
        acc = acc + _dot((u * u).astype(BF16), wd_ref[sl, :])
    y_ref[...] = _rms(acc, gf_ref[...])


def _mlp(x2d, g, wu, wd, gf, tm):
    rows, d = x2d.shape
    row = pl.BlockSpec((tm, d), lambda i: (i, 0))
    const = lambda a: pl.BlockSpec(a.shape, lambda i: (0, 0))
    return pl.pallas_call(
        _mlp_body, grid=(rows // tm,),
        in_specs=[row, const(g), const(wu), const(wd), const(gf)],
        out_specs=row, out_shape=jax.ShapeDtypeStruct(x2d.shape, F32),
        compiler_params=_cparams("arbitrary"), name="mlp")(x2d, g, wu, wd, gf)


def _row_tile(rows, want):
    return want if rows % want == 0 else rows


def kernel(x_prompt, x_sample, cache_a_k, cache_a_v, cache_b_k, cache_b_v, cache_b_logf, page_table,
           norm_attn_g, w_in, b_forget, rel_bias, lam_q1, lam_k1, lam_q2, lam_k2, subln_g,
           w_proj_a, w_proj_b, w_out, norm_mlp_g, w_up, w_down, norm_final_g):
    depth = w_in.shape[0]
    assert depth == 1, "one layer supported"
    B, T, D = x_prompt.shape
    DB, TS, _ = x_sample.shape
    assert TS == 1 and D == D_MODEL
    n_pool, page = cache_a_k.shape[1], cache_a_k.shape[2]

    w = w_in[0]
    splits = np.cumsum([W_QA, W_KA, W_VA, W_QB, W_KB, W_VB, H_B, D_MODEL])
    w_qa, w_ka, w_va, w_qb, w_kb, w_vb, w_f, w_ga, w_gb = jnp.split(w, [int(s) for s in splits], axis=1)
    w_qb = w_qb.reshape(D, H_B, DH_B)[:, _B_HEAD_PERM].reshape(D, W_QB)
    w_main = jnp.concatenate([w_qa, w_ka, w_va, w_qb, w_kb, w_vb, w_ga, w_gb], axis=1).astype(BF16)
    w_fl = jnp.pad(w_f, ((0, 0), (0, LANES - H_B))).astype(BF16)
    b_fl = b_forget[0].astype(F32).reshape(H_B, 1)
    g_attn = norm_attn_g[0].reshape(1, D).astype(F32)
    wpa = w_proj_a[0].astype(BF16)
    wpb = w_proj_b[0].reshape(H_B, DH_B, D)[_B_HEAD_PERM].reshape(H_B * DH_B, D).astype(BF16)
    wo = w_out[0].astype(BF16)
    wu = w_up[0].astype(BF16)
    wd = w_down[0].astype(BF16)
    g_mlp = norm_mlp_g[0].reshape(1, D).astype(F32)
    g_fin = norm_final_g.reshape(1, D).astype(F32)
    lq1, lk1, lq2, lk2 = (a[0].reshape(1, DH_A).astype(F32) for a in (lam_q1, lam_k1, lam_q2, lam_k2))
    sg = subln_g[0].reshape(1, 2 * DH_A).astype(F32)
    rb = rel_bias.astype(F32)

    xp = x_prompt.reshape(B * T, D)
    tm = _row_tile(B * T, 256)
    (qa, qb, ga, gb, ka_t, va4, kb_t, vb_t, lf_t, ka16, va16, kb16, vb16) = _in_proj(
        xp, g_attn, w_main, w_fl, b_fl, tm, T)
    c_t = _cumsum_t(lf_t)
    tq = _row_tile(T, 256)
    c4 = c_t.reshape(B, H_B // 4, 4, T)
    cq4 = c4.transpose(0, 1, 3, 2)
    ck4 = c4.reshape(B, H_B // 4, 4, T // tq, tq).transpose(0, 1, 3, 2, 4)
    oa = _attn_a(rb.reshape(-1), qa.reshape(B, T, W_QA), ka16.reshape(B, T, W_KA), va16.reshape(B, T, W_VA),
                 lq1, lk1, lq2, lk2, sg, tq)
    ob = _attn_b(qb.reshape(B, T, W_QB), kb16.reshape(B, T, W_KB), vb16.reshape(B, T, W_VB), cq4, ck4, tq)
    x1 = _merge(xp, oa.reshape(B * T, W_QA), ob.reshape(B * T, W_QB), ga, gb, wpa, wpb, wo, tm)
    y_prompt = _mlp(x1, g_mlp, wu, wd, g_fin, tm).reshape(B, T, D)

    xs = x_sample.reshape(DB, D)
    tms = _row_tile(DB, 128)
    (qa_s, qb_s, ga_s, gb_s, ka_st, va_s4, kb_st, vb_st, lf_st, _, _, _, _) = _in_proj(
        xs, g_attn, w_main, w_fl, b_fl, tms, DB)
    ka_s, kb_s, vb_s = ka_st[0].T, kb_st[0].T, vb_st[0].T
    va_s = va_s4.reshape(DB, W_VA)
    lf_s = lf_st[0].T
    qa5 = qa_s.reshape(DB, KV_A, G_A, 2, DH_A)
    wqa = jnp.einsum("bkgcd,kK,cC->bckgKCd", qa5, jnp.eye(KV_A, dtype=BF16), jnp.eye(2, dtype=BF16))
    wqa = wqa.reshape(DB, 2 * H_A, W_KA)
    qb4 = qb_s.reshape(DB, H_B, DH_B)[:, _B_HEAD_INV].reshape(DB, KV_B, G_B, DH_B)
    wqb = jnp.einsum("bkgd,kK->bkgKd", qb4, jnp.eye(KV_B, dtype=BF16)).reshape(DB, H_B, W_KB)
    pps = math.gcd(page_table.shape[1], 8)
    rb_t = jnp.tile(rb.T, (2, 1))
    oa_s, ob_raw = _decode(
        page_table, wqa, wqb, ka_s.reshape(DB, 1, W_KA), va_s.reshape(DB, 1, W_VA),
        kb_s.reshape(DB, 1, W_KB), vb_s.reshape(DB, 1, W_VB), lf_s.reshape(DB, H_B, 1), rb_t,
        cache_a_k[0].transpose(0, 2, 3, 4, 1).reshape(n_pool, W_KA, page),
        cache_a_v[0].reshape(n_pool, page * KV_A, 2 * DH_A),
        cache_b_k[0].transpose(0, 2, 3, 1).reshape(n_pool, W_KB, page),
        cache_b_v[0].transpose(0, 2, 3, 1).reshape(n_pool, W_VB, page),
        cache_b_logf[0].transpose(0, 2, 1), lq1, lk1, lq2, lk2, sg, pps)
    odd_kv = ((jnp.arange(H_B) // G_B) % 2 == 1)[None, :, None]
    ob_s = jnp.where(odd_kv, ob_raw[..., DH_B:], ob_raw[..., :DH_B])
    ob_s = ob_s[:, _B_HEAD_PERM].reshape(DB, W_QB).astype(BF16)
    x1_s = _merge(xs, oa_s.reshape(DB, W_QA).astype(BF16), ob_s, ga_s, gb_s, wpa, wpb, wo, tms)
    y_sample = _mlp(x1_s, g_mlp, wu, wd, g_fin, tms).reshape(DB, 1, D)

    return (y_prompt, y_sample,
            ka_t.reshape(1, B, KV_A, 2, DH_A, T).transpose(0, 1, 5, 2, 3, 4),
            va4.reshape(1, B, T, KV_A, 2 * DH_A),
            kb_t.reshape(1, B, KV_B, DH_B, T).transpose(0, 1, 4, 2, 3),
            vb_t.reshape(1, B, KV_B, DH_B, T).transpose(0, 1, 4, 2, 3),
            lf_t.reshape(1, B, H_B, T).transpose(0, 1, 3, 2),
            ka_s.reshape(1, DB, 1, KV_A, 2, DH_A), va_s.reshape(1, DB, 1, KV_A, 2 * DH_A),
            kb_s.reshape(1, DB, 1, KV_B, DH_B), vb_s.reshape(1, DB, 1, KV_B, DH_B),
            lf_s.reshape(1, DB, 1, H_B))
```

```python
import functools
import math

import jax
import jax.numpy as jnp
import numpy as np
from jax import lax
from jax.experimental import pallas as pl
from jax.experimental.pallas import tpu as pltpu

F32 = jnp.float32
BF16 = jnp.bfloat16

D_MODEL = 1024
H_A, KV_A, DH_A = 8, 4, 64
G_A = H_A // KV_A
H_B, KV_B, DH_B = 16, 8, 64
G_B = H_B // KV_B
D_FF = 4 * D_MODEL
N_BUCKETS = 32
MAX_DISTANCE = 128
EPS = 1e-6
NEG = -1e30
LAM_INIT = 0.8 - 0.6 * math.exp(-0.3 * 0)
LOG2E = math.log2(math.e)
Q_SCALE = DH_A ** -0.5 * LOG2E
assert DH_A == DH_B and G_A == 2 and G_B == 2

LANES = 128
VMEM_LIMIT_BYTES = 56 * 1024 * 1024

W_QA, W_KA, W_VA = H_A * 2 * DH_A, KV_A * 2 * DH_A, KV_A * 2 * DH_A
W_QB, W_KB, W_VB = H_B * DH_B, KV_B * DH_B, KV_B * DH_B
_OFFS = np.cumsum([0, W_QA, W_KA, W_VA, W_QB, W_KB, W_VB, D_MODEL, D_MODEL])
N_MAIN = int(_OFFS[-1])

_B_HEAD_PERM = np.array([4 * i + j for i in range(H_B // 4) for j in (0, 2, 1, 3)])
_B_HEAD_INV = np.argsort(_B_HEAD_PERM)


def _cparams(*sem):
    return pltpu.CompilerParams(dimension_semantics=sem, vmem_limit_bytes=VMEM_LIMIT_BYTES)


def _rms(x, g):
    return (x * lax.rsqrt(jnp.mean(x * x, axis=-1, keepdims=True) + EPS)) * g


def _t5_bucket(dist):
    n = jnp.maximum(dist, 0)
    max_exact = N_BUCKETS // 2
    nf = jnp.maximum(n, max_exact).astype(F32)
    large = max_exact + (jnp.log(nf / max_exact) / math.log(MAX_DISTANCE / max_exact)
                         * (N_BUCKETS - max_exact)).astype(jnp.int32)
    large = jnp.minimum(large, N_BUCKETS - 1)
    return jnp.where(n < max_exact, n, large)


def _split3(x):
    hi = x.astype(BF16)
    r1 = x - hi.astype(F32)
    mid = r1.astype(BF16)
    lo = (r1 - mid.astype(F32)).astype(BF16)
    return hi, mid, lo


def _dot(a, b):
    return jnp.dot(a, b, preferred_element_type=F32)


def _dot_nt(a, b):
    return lax.dot_general(a, b, (((1,), (1,)), ((), ())), preferred_element_type=F32)


def _in_proj_body(x_ref, g_ref, w_ref, wfl_ref, bfl_ref,
                  qa_ref, qb_ref, ga_ref, gb_ref, kat_ref, va4_ref, kbt_ref, vbt_ref, lft_ref,
                  ka16_ref, va16_ref, kb16_ref, vb16_ref):
    xb = _rms(x_ref[...], g_ref[...]).astype(BF16)
    tm = xb.shape[0]

    def seg(i):
        return _dot(xb, w_ref[:, int(_OFFS[i]):int(_OFFS[i + 1])])

    qa_ref[...] = (seg(0) * Q_SCALE).astype(BF16)
    for i, ot, o16 in ((1, kat_ref, ka16_ref), (4, kbt_ref, kb16_ref), (5, vbt_ref, vb16_ref)):
        z = seg(i)
        ot[0] = z.T
        o16[...] = z.astype(BF16)
    z = seg(2)
    va16_ref[...] = z.astype(BF16)
    for kv in range(KV_A):
        va4_ref[0, pl.ds(kv, tm, stride=KV_A), :] = z[:, kv * 2 * DH_A:(kv + 1) * 2 * DH_A]
    qb_ref[...] = (seg(3) * Q_SCALE).astype(BF16)
    ga_ref[...] = (1.0 / (1.0 + jnp.exp(-seg(6)))).astype(BF16)
    gb_ref[...] = (1.0 / (1.0 + jnp.exp(-seg(7)))).astype(BF16)
    zf = _dot(xb, wfl_ref[...]).T[:H_B] + bfl_ref[...]
    lft_ref[0] = jnp.minimum(zf, 0.0) - jnp.log1p(jnp.exp(-jnp.abs(zf)))


def _in_proj(x2d, g, w_main, w_fl, b_fl, tm, seq):
    rows, d = x2d.shape
    nb, nt = rows // seq, seq // tm
    row = lambda n: pl.BlockSpec((tm, n), lambda i: (i, 0))
    const = lambda a: pl.BlockSpec(a.shape, lambda i: (0, 0))
    tspec = lambda n: pl.BlockSpec((1, n, tm), lambda i: (i // nt, 0, i % nt))
    widths16 = (W_QA, W_QB, D_MODEL, D_MODEL)
    widths32 = (W_KA, W_VA, W_KB, W_VB)
    out_shape = ([jax.ShapeDtypeStruct((rows, n), BF16) for n in widths16]
                 + [jax.ShapeDtypeStruct((nb, W_KA, seq), F32),
                    jax.ShapeDtypeStruct((nb, seq * KV_A, 2 * DH_A), F32),
                    jax.ShapeDtypeStruct((nb, W_KB, seq), F32),
                    jax.ShapeDtypeStruct((nb, W_VB, seq), F32),
                    jax.ShapeDtypeStruct((nb, H_B, seq), F32)]
                 + [jax.ShapeDtypeStruct((rows, n), BF16) for n in widths32])
    out_specs = ([row(n) for n in widths16]
                 + [tspec(W_KA), pl.BlockSpec((1, tm * KV_A, 2 * DH_A), lambda i: (i // nt, i % nt, 0)),
                    tspec(W_KB), tspec(W_VB), tspec(H_B)]
                 + [row(n) for n in widths32])
    return pl.pallas_call(
        _in_proj_body, grid=(rows // tm,),
        in_specs=[row(d), const(g), const(w_main), const(w_fl), const(b_fl)],
        out_specs=out_specs, out_shape=out_shape,
        compiler_params=_cparams("arbitrary"), name="in_proj")(x2d, g, w_main, w_fl, b_fl)


_CS_BLK = 256


def _cumsum_body(lf_ref, c_ref):
    t = lf_ref.shape[2]
    r = lax.broadcasted_iota(jnp.int32, (_CS_BLK, _CS_BLK), 0)
    c = lax.broadcasted_iota(jnp.int32, (_CS_BLK, _CS_BLK), 1)
    tri = (r <= c).astype(F32).astype(BF16)
    carry = jnp.zeros((H_B, 1), F32)
    for i in range(t // _CS_BLK):
        sl = slice(i * _CS_BLK, (i + 1) * _CS_BLK)
        hi, mid, lo = _split3(lf_ref[0, :, sl])
        cs = _dot(hi, tri) + _dot(mid, tri) + _dot(lo, tri) + carry
        c_ref[0, :, sl] = cs
        carry = cs[:, _CS_BLK - 1:_CS_BLK]


def _cumsum_t(lf_t):
    b, h, t = lf_t.shape
    assert t % _CS_BLK == 0
    spec = pl.BlockSpec((1, h, t), lambda i: (i, 0, 0))
    return pl.pallas_call(
        _cumsum_body, grid=(b,), in_specs=[spec], out_specs=spec,
        out_shape=jax.ShapeDtypeStruct(lf_t.shape, F32),
        compiler_params=_cparams("arbitrary"), name="cumsum")(lf_t)


def _lam(lq1_ref, lk1_ref, lq2_ref, lk2_ref):
    return (jnp.exp(jnp.sum(lq1_ref[...] * lk1_ref[...], keepdims=True))
            - jnp.exp(jnp.sum(lq2_ref[...] * lk2_ref[...], keepdims=True)) + LAM_INIT)


def _lane_chunks(x):
    return [x[:, j * LANES:(j + 1) * LANES] for j in range(x.shape[1] // LANES)]


def _keep_scores(kb, nb, tq, s, s_ref, mpart_ref, first=False):
    for j in range(nb):
        s_ref[kb + j] = s[:, j * tq:(j + 1) * tq]
    chunks = _lane_chunks(s)
    mp = chunks[0] if first else jnp.maximum(mpart_ref[...], chunks[0])
    for sj in chunks[1:]:
        mp = jnp.maximum(mp, sj)
    mpart_ref[...] = mp


def _paired(n, fn):
    def body(i, carry):
        fn(4 * i, 4)
        return carry

    lax.fori_loop(0, n // 4, body, 0)

    @pl.when(n % 4 >= 2)
    def _():
        fn((n // 4) * 4, 2)

    @pl.when(n % 2 == 1)
    def _():
        fn(n - 1, 1)


def _value_pass(last, tq, s_ref, mb_ref, v_ref, acc_ref):
    def blocks(kb, nb, first=False):
        mb = mb_ref[...]
        ps = [jnp.exp2(sj - mb) for j in range(nb) for sj in _lane_chunks(s_ref[kb + j])]
        p16 = jnp.concatenate(ps, axis=1).astype(BF16)
        rows = pl.ds(pl.multiple_of(kb * tq, tq), nb * tq)
        v1 = jnp.concatenate([v_ref[0, rows, :], jnp.ones((nb * tq, LANES), BF16)], axis=1)
        pv = _dot(p16, v1)
        acc_ref[...] = pv if first else acc_ref[...] + pv

    blocks(last, 1, True)
    _paired(last, blocks)
    acc = acc_ref[...]
    return acc[:, :LANES] / acc[:, LANES:]


def _attn_a_body(rb_ref, q_ref, k_ref, v_ref, lq1_ref, lk1_ref, lq2_ref, lk2_ref, sg_ref,
                 o_ref, bias_ref, bucket_ref, q4_ref, s_ref, mpart_ref, mb_ref, acc_ref, *, tq):
    b, kvh, qi = pl.program_id(0), pl.program_id(1), pl.program_id(2)
    far = (N_BUCKETS - 1) * H_A

    @pl.when((b == 0) & (kvh == 0) & (qi == 0))
    def _():
        r = lax.broadcasted_iota(jnp.int32, (tq, tq), 0)
        c = lax.broadcasted_iota(jnp.int32, (tq, tq), 1)
        for t in range(2):
            dist = r - c + t * tq
            bucket_ref[...] = _t5_bucket(dist)
            for h in range(H_A):
                bias_ref[h, t] = jnp.zeros((tq, tq), F32)

                def body(i, carry, h=h, t=t):
                    rel = (rb_ref[i * H_A + h] - rb_ref[far + h]) * LOG2E
                    bias_ref[h, t] = jnp.where(bucket_ref[...] == i, rel, bias_ref[h, t])
                    return carry

                lax.fori_loop(0, N_BUCKETS, body, 0)
                if t == 0:
                    bias_ref[h, t] = jnp.where(dist >= 0, bias_ref[h, t], NEG)

    lane = lax.broadcasted_iota(jnp.int32, (tq, LANES), 1)
    for g in range(G_A):
        qg = q_ref[0, :, g * LANES:(g + 1) * LANES].astype(F32)
        for c in range(2):
            keep = (lane >= DH_A) if c else (lane < DH_A)
            q4_ref[pl.ds((2 * g + c) * tq, tq), :] = jnp.where(keep, qg, 0.0).astype(BF16)

    def score_pass(kb, nb, near=None):
        rows = pl.ds(pl.multiple_of(kb * tq, tq), nb * tq)
        s = _dot_nt(q4_ref[...], k_ref[0, rows, :])
        if near is not None:
            s = jnp.concatenate([s[i * tq:(i + 1) * tq] + bias_ref[kvh * G_A + i // 2, near]
                                 for i in range(4)], axis=0)
        _keep_scores(kb, nb, tq, s, s_ref, mpart_ref, first=(near == 0))

    score_pass(qi, 1, 0)

    @pl.when(qi >= 1)
    def _():
        score_pass(qi - 1, 1, 1)

    _paired(jnp.maximum(qi - 1, 0), score_pass)
    mb_ref[...] = jnp.broadcast_to(jnp.max(mpart_ref[...], axis=-1, keepdims=True), mb_ref.shape)
    o_all = _value_pass(qi, tq, s_ref, mb_ref, v_ref, acc_ref)

    lam = _lam(lq1_ref, lk1_ref, lq2_ref, lk2_ref)
    for g in range(G_A):
        o = o_all[2 * g * tq:(2 * g + 1) * tq] - lam * o_all[(2 * g + 1) * tq:(2 * g + 2) * tq]
        y = _rms(o, sg_ref[...]) * (1.0 - LAM_INIT)
        o_ref[0, :, g * LANES:(g + 1) * LANES] = y.astype(BF16)


def _attn_scratch(tq, nq):
    m = 4 * tq
    return [pltpu.VMEM((m, LANES), BF16), pltpu.VMEM((nq, m, tq), F32), pltpu.VMEM((m, LANES), F32),
            pltpu.VMEM((m, LANES), F32), pltpu.VMEM((m, 2 * LANES), F32)]


def _attn_a(rb_flat, qa, ka16, va16, lq1, lk1, lq2, lk2, sg, tq):
    b, t, _ = qa.shape
    assert t % tq == 0 and tq >= MAX_DISTANCE and tq % LANES == 0
    wq = G_A * 2 * DH_A
    qspec = pl.BlockSpec((1, tq, wq), lambda bi, k, i: (bi, i, k))
    kvspec = pl.BlockSpec((1, t, 2 * DH_A), lambda bi, k, i: (bi, 0, k))
    small = lambda a: pl.BlockSpec(a.shape, lambda bi, k, i: (0, 0))
    return pl.pallas_call(
        functools.partial(_attn_a_body, tq=tq), grid=(b, KV_A, t // tq),
        in_specs=[pl.BlockSpec(memory_space=pltpu.SMEM), qspec, kvspec, kvspec,
                  small(lq1), small(lk1), small(lq2), small(lk2), small(sg)],
        out_specs=qspec, out_shape=jax.ShapeDtypeStruct(qa.shape, BF16),
        scratch_shapes=[pltpu.VMEM((H_A, 2, tq, tq), F32), pltpu.VMEM((tq, tq), jnp.int32)]
        + _attn_scratch(tq, t // tq),
        compiler_params=_cparams("arbitrary", "arbitrary", "arbitrary"), name="attn_a",
    )(rb_flat, qa, ka16, va16, lq1, lk1, lq2, lk2, sg)


def _attn_b_body(q_ref, k_ref, v_ref, cq_ref, ck_ref, o_ref,
                 q4_ref, s_ref, mpart_ref, mb_ref, acc_ref, *, tq):
    qi = pl.program_id(2)
    lane = lax.broadcasted_iota(jnp.int32, (tq, LANES), 1)
    heads = [2 * (i % 2) + i // 2 for i in range(4)]
    for s_ in range(2):
        qs = q_ref[0, :, s_ * LANES:(s_ + 1) * LANES].astype(F32)
        for half in range(2):
            keep = (lane >= DH_B) if half else (lane < DH_B)
            q4_ref[pl.ds((2 * s_ + half) * tq, tq), :] = jnp.where(keep, qs, 0.0).astype(BF16)
    r = lax.broadcasted_iota(jnp.int32, (tq, tq), 0)
    c = lax.broadcasted_iota(jnp.int32, (tq, tq), 1)
    causal = r >= c

    def score_pass(kb, nb, diag=False):
        rows = pl.ds(pl.multiple_of(kb * tq, tq), nb * tq)
        s = _dot_nt(q4_ref[...], k_ref[0, rows, :])
        parts = []
        for i in range(4):
            ck = jnp.concatenate([ck_ref[0, 0, kb + j, heads[i]:heads[i] + 1, :] for j in range(nb)], axis=1)
            si = s[i * tq:(i + 1) * tq] - ck * LOG2E
            parts.append(jnp.where(causal, si, NEG) if diag else si)
        _keep_scores(kb, nb, tq, jnp.concatenate(parts, axis=0), s_ref, mpart_ref, first=diag)

    score_pass(qi, 1, True)
    _paired(qi, score_pass)
    cq = jnp.concatenate([cq_ref[0, 0, :, heads[i]:heads[i] + 1] for i in range(4)], axis=0) * LOG2E
    m = jnp.max(mpart_ref[...], axis=-1, keepdims=True) + cq
    mb_ref[...] = jnp.broadcast_to(m - cq, mb_ref.shape)
    o_all = _value_pass(qi, tq, s_ref, mb_ref, v_ref, acc_ref)
    for s_ in range(2):
        o0 = o_all[2 * s_ * tq:(2 * s_ + 1) * tq]
        o1 = o_all[(2 * s_ + 1) * tq:(2 * s_ + 2) * tq]
        o_ref[0, :, s_ * LANES:(s_ + 1) * LANES] = jnp.where(lane < DH_B, o0, o1).astype(BF16)


def _attn_b(qb, kb16, vb16, cq4, ck4, tq):
    b, t, _ = qb.shape
    npair = KV_B // 2
    qspec = pl.BlockSpec((1, tq, 2 * LANES), lambda bi, k, i: (bi, i, k))
    kvspec = pl.BlockSpec((1, t, LANES), lambda bi, k, i: (bi, 0, k))
    cqspec = pl.BlockSpec((1, 1, tq, 4), lambda bi, k, i: (bi, k, i, 0))
    ckspec = pl.BlockSpec((1, 1, t // tq, 4, tq), lambda bi, k, i: (bi, k, 0, 0, 0))
    return pl.pallas_call(
        functools.partial(_attn_b_body, tq=tq), grid=(b, npair, t // tq),
        in_specs=[qspec, kvspec, kvspec, cqspec, ckspec],
        out_specs=qspec, out_shape=jax.ShapeDtypeStruct(qb.shape, BF16),
        scratch_shapes=_attn_scratch(tq, t // tq),
        compiler_params=_cparams("arbitrary", "arbitrary", "arbitrary"), name="attn_b",
    )(qb, kb16, vb16, cq4, ck4)


def _decode_body(pt_ref, wqa_ref, wqb_ref, kan_ref, van_ref, kbn_ref, vbn_ref, lfn_ref, rbt_ref, *rest,
                 page, pps):
    del pt_ref
    cak, cav, cbk, cbv, clf = (rest[i * pps:(i + 1) * pps] for i in range(5))
    (lq1_ref, lk1_ref, lq2_ref, lk2_ref, sg_ref, oa_ref, ob_ref,
     bias_ref, m_ref, l_ref, acc_ref, carry_ref) = rest[5 * pps:]
    p = pl.program_id(1)
    nrow = 2 * H_A

    @pl.when(p == 0)
    def _():
        sa = jnp.sum(wqa_ref[0].astype(F32) * kan_ref[0], axis=-1, keepdims=True) + rbt_ref[:, 0:1] * LOG2E
        sb = jnp.sum(wqb_ref[0].astype(F32) * kbn_ref[0], axis=-1, keepdims=True)
        m_ref[0] = jnp.broadcast_to(sa, m_ref.shape[1:])
        m_ref[1] = jnp.broadcast_to(sb, m_ref.shape[1:])
        l_ref[...] = jnp.ones(l_ref.shape, F32)
        acc_ref[0] = jnp.broadcast_to(van_ref[0], acc_ref.shape[1:])
        acc_ref[1] = jnp.broadcast_to(vbn_ref[0], acc_ref.shape[1:])
        carry_ref[...] = lfn_ref[0]
        dist = page - lax.broadcasted_iota(jnp.int32, (nrow, page), 1)
        bucket = _t5_bucket(dist)
        acc = jnp.zeros((nrow, page), F32)
        for i in range(N_BUCKETS):
            acc = jnp.where(bucket == i, rbt_ref[:, i:i + 1], acc)
        bias_ref[...] = jnp.broadcast_to(rbt_ref[:, N_BUCKETS - 1:N_BUCKETS] * LOG2E, bias_ref.shape)
        bias_ref[:, 0:page] = acc * LOG2E

    @pl.when(p == 1)
    def _():
        bias_ref[...] = jnp.broadcast_to(rbt_ref[:, N_BUCKETS - 1:N_BUCKETS] * LOG2E, bias_ref.shape)

    def update(idx, s, pv):
        m_prev = m_ref[idx]
        m_new = jnp.maximum(m_prev, jnp.max(s, axis=-1, keepdims=True))
        alpha = jnp.exp2(m_prev - m_new)
        pr = jnp.exp2(s - m_new[:, 0:1])
        l_ref[idx] = alpha * l_ref[idx] + jnp.sum(pr, axis=-1, keepdims=True)
        acc_ref[idx] = alpha[:, 0:1] * acc_ref[idx] + pv(pr.astype(BF16))
        m_ref[idx] = m_new

    kt_a = jnp.concatenate([r[0].astype(BF16) for r in cak], axis=1)
    s_a = _dot(wqa_ref[0], kt_a) + bias_ref[...]
    v_a = jnp.concatenate(
        [jnp.concatenate([r[0, pl.ds(kv, page, stride=KV_A), :] for kv in range(KV_A)], axis=1)
         for r in cav], axis=0).astype(BF16)
    update(0, s_a, lambda pr: _dot(pr, v_a))

    r = lax.broadcasted_iota(jnp.int32, (page, page), 0)
    c = lax.broadcasted_iota(jnp.int32, (page, page), 1)
    later = (r > c).astype(F32).astype(BF16)
    lfs = [ref[0] for ref in clf]
    suf = _dot(jnp.concatenate([piece for lf in lfs for piece in _split3(lf)], axis=0), later)
    carry = carry_ref[...]
    decay = []
    for j, lf in enumerate(lfs):
        o = 3 * H_B * j
        suffix = suf[o:o + H_B] + suf[o + H_B:o + 2 * H_B] + suf[o + 2 * H_B:o + 3 * H_B]
        decay.append(suffix + carry)
        carry = carry + suffix[:, 0:1] + lf[:, 0:1]
    carry_ref[...] = carry
    kt_b = jnp.concatenate([ref[0].astype(BF16) for ref in cbk], axis=1)
    s_b = _dot(wqb_ref[0], kt_b) + jnp.concatenate(decay, axis=1) * LOG2E
    vt_b = jnp.concatenate([ref[0].astype(BF16) for ref in cbv], axis=1)
    update(1, s_b, lambda pr: _dot_nt(pr, vt_b))

    @pl.when(p == pl.num_programs(1) - 1)
    def _():
        rowi = lax.broadcasted_iota(jnp.int32, acc_ref.shape[1:], 0)
        lanei = lax.broadcasted_iota(jnp.int32, acc_ref.shape[1:], 1)

        def own_block(o, keep):
            o = jnp.where(keep, o, 0.0)
            return functools.reduce(lambda a, b: a + b, _lane_chunks(o))

        oa = own_block(acc_ref[0] / l_ref[0][:, 0:1], (lanei // (2 * DH_A)) == ((rowi % H_A) // G_A))
        lam = _lam(lq1_ref, lk1_ref, lq2_ref, lk2_ref)
        o = oa[0:H_A] - lam * oa[H_A:2 * H_A]
        oa_ref[0] = _rms(o, sg_ref[...]) * (1.0 - LAM_INIT)
        ob_ref[0] = own_block(acc_ref[1] / l_ref[1][:, 0:1], (lanei // DH_B) == (rowi // G_B))


def _decode(page_table, wqa, wqb, ka_new, va_new, kb_new, vb_new, lf_new, rb_t,
            cak_t, cav2, cbk_t, cbv_t, clf_t, lq1, lk1, lq2, lk2, sg, pps):
    db, n_pages = page_table.shape
    page = cak_t.shape[2]
    width = cak_t.shape[1]
    assert page >= MAX_DISTANCE and page == LANES and n_pages % pps == 0
    per_b = lambda a: pl.BlockSpec((1,) + a.shape[1:], lambda b, p, pt: (b, 0, 0))
    small = lambda a: pl.BlockSpec(a.shape, lambda b, p, pt: (0, 0))

    def paged(a, j):
        return pl.BlockSpec((1,) + a.shape[1:],
                            lambda b, p, pt: (pt[b * n_pages + n_pages - 1 - (p * pps + j)], 0, 0))

    nrow = 2 * H_A
    caches = (cak_t, cav2, cbk_t, cbv_t, clf_t)
    grid_spec = pltpu.PrefetchScalarGridSpec(
        num_scalar_prefetch=1, grid=(db, n_pages // pps),
        in_specs=[per_b(wqa), per_b(wqb), per_b(ka_new), per_b(va_new), per_b(kb_new), per_b(vb_new),
                  per_b(lf_new), small(rb_t)]
        + [paged(a, j) for a in caches for j in range(pps)]
        + [small(lq1), small(lk1), small(lq2), small(lk2), small(sg)],
        out_specs=[pl.BlockSpec((1, H_A, 2 * DH_A), lambda b, p, pt: (b, 0, 0)),
                   pl.BlockSpec((1, H_B, LANES), lambda b, p, pt: (b, 0, 0))],
        scratch_shapes=[pltpu.VMEM((nrow, pps * page), F32), pltpu.VMEM((2, nrow, LANES), F32),
                        pltpu.VMEM((2, nrow, LANES), F32), pltpu.VMEM((2, nrow, width), F32),
                        pltpu.VMEM((H_B, 1), F32)])
    return pl.pallas_call(
        functools.partial(_decode_body, page=page, pps=pps), grid_spec=grid_spec,
        out_shape=[jax.ShapeDtypeStruct((db, H_A, 2 * DH_A), F32),
                   jax.ShapeDtypeStruct((db, H_B, LANES), F32)],
        compiler_params=_cparams("arbitrary", "arbitrary"), name="decode",
    )(page_table.reshape(-1), wqa, wqb, ka_new, va_new, kb_new, vb_new, lf_new, rb_t,
      *[a for a in caches for _ in range(pps)], lq1, lk1, lq2, lk2, sg)


def _merge_body(x_ref, oa_ref, ob_ref, ga_ref, gb_ref, wpa_ref, wpb_ref, wo_ref, x1_ref):
    pa = _dot(oa_ref[...], wpa_ref[...])
    pb = _dot(ob_ref[...], wpb_ref[...])
    h = ga_ref[...].astype(F32) * pa + gb_ref[...].astype(F32) * pb
    x1_ref[...] = x_ref[...] + _dot(h.astype(BF16), wo_ref[...])


def _merge(x2d, oa, ob, ga, gb, wpa, wpb, wo, tm):
    rows, d = x2d.shape
    row = pl.BlockSpec((tm, d), lambda i: (i, 0))
    const = lambda a: pl.BlockSpec(a.shape, lambda i: (0, 0))
    return pl.pallas_call(
        _merge_body, grid=(rows // tm,),
        in_specs=[row, row, row, row, row, const(wpa), const(wpb), const(wo)],
        out_specs=row, out_shape=jax.ShapeDtypeStruct(x2d.shape, F32),
        compiler_params=_cparams("arbitrary"), name="merge")(x2d, oa, ob, ga, gb, wpa, wpb, wo)


_FF_CHUNK = 1024


def _mlp_body(x_ref, g_ref, wu_ref, wd_ref, gf_ref, y_ref):
    x = x_ref[...]
    hn = _rms(x, g_ref[...]).astype(BF16)
    acc = x
    for c in range(D_FF // _FF_CHUNK):
        sl = slice(c * _FF_CHUNK, (c + 1) * _FF_CHUNK)
        u = jnp.maximum(_dot(hn, wu_ref[:, sl]), 0.0)
        acc = acc + _dot((u * u).astype(BF16), wd_ref[sl, :])
    y_ref[...] = _rms(acc, gf_ref[...])


def _mlp(x2d, g, wu, wd, gf, tm):
    rows, d = x2d.shape
    row = pl.BlockSpec((tm, d), lambda i: (i, 0))
    const = lambda a: pl.BlockSpec(a.shape, lambda i: (0, 0))
    return pl.pallas_call(
        _mlp_body, grid=(rows // tm,),
        in_specs=[row, const(g), const(wu), const(wd), const(gf)],
        out_specs=row, out_shape=jax.ShapeDtypeStruct(x2d.shape, F32),
        compiler_params=_cparams("arbitrary"), name="mlp")(x2d, g, wu, wd, gf)


def _row_tile(rows, want):
    return want if rows % want == 0 else rows


def kernel(x_prompt, x_sample, cache_a_k, cache_a_v, cache_b_k, cache_b_v, cache_b_logf, page_table,
           norm_attn_g, w_in, b_forget, rel_bias, lam_q1, lam_k1, lam_q2, lam_k2, subln_g,
           w_proj_a, w_proj_b, w_out, norm_mlp_g, w_up, w_down, norm_final_g):
    depth = w_in.shape[0]
    assert depth == 1, "one layer supported"
    B, T, D = x_prompt.shape
    DB, TS, _ = x_sample.shape
    assert TS == 1 and D == D_MODEL
    n_pool, page = cache_a_k.shape[1], cache_a_k.shape[2]

    w = w_in[0]
    splits = np.cumsum([W_QA, W_KA, W_VA, W_QB, W_KB, W_VB, H_B, D_MODEL])
    w_qa, w_ka, w_va, w_qb, w_kb, w_vb, w_f, w_ga, w_gb = jnp.split(w, [int(s) for s in splits], axis=1)
    w_qb = w_qb.reshape(D, H_B, DH_B)[:, _B_HEAD_PERM].reshape(D, W_QB)
    w_main = jnp.concatenate([w_qa, w_ka, w_va, w_qb, w_kb, w_vb, w_ga, w_gb], axis=1).astype(BF16)
    w_fl = jnp.pad(w_f, ((0, 0), (0, LANES - H_B))).astype(BF16)
    b_fl = b_forget[0].astype(F32).reshape(H_B, 1)
    g_attn = norm_attn_g[0].reshape(1, D).astype(F32)
    wpa = w_proj_a[0].astype(BF16)
    wpb = w_proj_b[0].reshape(H_B, DH_B, D)[_B_HEAD_PERM].reshape(H_B * DH_B, D).astype(BF16)
    wo = w_out[0].astype(BF16)
    wu = w_up[0].astype(BF16)
    wd = w_down[0].astype(BF16)
    g_mlp = norm_mlp_g[0].reshape(1, D).astype(F32)
    g_fin = norm_final_g.reshape(1, D).astype(F32)
    lq1, lk1, lq2, lk2 = (a[0].reshape(1, DH_A).astype(F32) for a in (lam_q1, lam_k1, lam_q2, lam_k2))
    sg = subln_g[0].reshape(1, 2 * DH_A).astype(F32)
    rb = rel_bias.astype(F32)

    xp = x_prompt.reshape(B * T, D)
    tm = _row_tile(B * T, 256)
    (qa, qb, ga, gb, ka_t, va4, kb_t, vb_t, lf_t, ka16, va16, kb16, vb16) = _in_proj(
        xp, g_attn, w_main, w_fl, b_fl, tm, T)
    c_t = _cumsum_t(lf_t)
    tq = _row_tile(T, 256)
    c4 = c_t.reshape(B, H_B // 4, 4, T)
    cq4 = c4.transpose(0, 1, 3, 2)
    ck4 = c4.reshape(B, H_B // 4, 4, T // tq, tq).transpose(0, 1, 3, 2, 4)
    oa = _attn_a(rb.reshape(-1), qa.reshape(B, T, W_QA), ka16.reshape(B, T, W_KA), va16.reshape(B, T, W_VA),
                 lq1, lk1, lq2, lk2, sg, tq)
    ob = _attn_b(qb.reshape(B, T, W_QB), kb16.reshape(B, T, W_KB), vb16.reshape(B, T, W_VB), cq4, ck4, tq)
    x1 = _merge(xp, oa.reshape(B * T, W_QA), ob.reshape(B * T, W_QB), ga, gb, wpa, wpb, wo, tm)
    y_prompt = _mlp(x1, g_mlp, wu, wd, g_fin, tm).reshape(B, T, D)

    xs = x_sample.reshape(DB, D)
    tms = _row_tile(DB, 128)
    (qa_s, qb_s, ga_s, gb_s, ka_st, va_s4, kb_st, vb_st, lf_st, _, _, _, _) = _in_proj(
        xs, g_attn, w_main, w_fl, b_fl, tms, DB)
    ka_s, kb_s, vb_s = ka_st[0].T, kb_st[0].T, vb_st[0].T
    va_s = va_s4.reshape(DB, W_VA)
    lf_s = lf_st[0].T
    qa5 = qa_s.reshape(DB, KV_A, G_A, 2, DH_A)
    wqa = jnp.einsum("bkgcd,kK,cC->bckgKCd", qa5, jnp.eye(KV_A, dtype=BF16), jnp.eye(2, dtype=BF16))
    wqa = wqa.reshape(DB, 2 * H_A, W_KA)
    qb4 = qb_s.reshape(DB, H_B, DH_B)[:, _B_HEAD_INV].reshape(DB, KV_B, G_B, DH_B)
    wqb = jnp.einsum("bkgd,kK->bkgKd", qb4, jnp.eye(KV_B, dtype=BF16)).reshape(DB, H_B, W_KB)
    pps = math.gcd(page_table.shape[1], 16)
    rb_t = jnp.tile(rb.T, (2, 1))
    oa_s, ob_raw = _decode(
        page_table, wqa, wqb, ka_s.reshape(DB, 1, W_KA), va_s.reshape(DB, 1, W_VA),
        kb_s.reshape(DB, 1, W_KB), vb_s.reshape(DB, 1, W_VB), lf_s.reshape(DB, H_B, 1), rb_t,
        cache_a_k[0].transpose(0, 2, 3, 4, 1).reshape(n_pool, W_KA, page),
        cache_a_v[0].reshape(n_pool, page * KV_A, 2 * DH_A),
        cache_b_k[0].transpose(0, 2, 3, 1).reshape(n_pool, W_KB, page),
        cache_b_v[0].transpose(0, 2, 3, 1).reshape(n_pool, W_VB, page),
        cache_b_logf[0].transpose(0, 2, 1), lq1, lk1, lq2, lk2, sg, pps)
    odd_kv = ((jnp.arange(H_B) // G_B) % 2 == 1)[None, :, None]
    ob_s = jnp.where(odd_kv, ob_raw[..., DH_B:], ob_raw[..., :DH_B])
    ob_s = ob_s[:, _B_HEAD_PERM].reshape(DB, W_QB).astype(BF16)
    x1_s = _merge(xs, oa_s.reshape(DB, W_QA).astype(BF16), ob_s, ga_s, gb_s, wpa, wpb, wo, tms)
    y_sample = _mlp(x1_s, g_mlp, wu, wd, g_fin, tms).reshape(DB, 1, D)

    return (y_prompt, y_sample,
            ka_t.reshape(1, B, KV_A, 2, DH_A, T).transpose(0, 1, 5, 2, 3, 4),
            va4.reshape(1, B, T, KV_A, 2 * DH_A),
            kb_t.reshape(1, B, KV_B, DH_B, T).transpose(0, 1, 4, 2, 3),
            vb_t.reshape(1, B, KV_B, DH_B, T).transpose(0, 1, 4, 2, 3),
            lf_t.reshape(1, B, H_B, T).transpose(0, 1, 3, 2),
            ka_s.reshape(1, DB, 1, KV_A, 2, DH_A), va_s.reshape(1, DB, 1, KV_A, 2 * DH_A),
            kb_s.reshape(1, DB, 1, KV_B, DH_B), vb_s.reshape(1, DB, 1, KV_B, DH_B),
            lf_s.reshape(1, DB, 1, H_B))
```

```python
import functools
import math

import jax
import jax.numpy as jnp
import numpy as np
from jax import lax
from jax.experimental import pallas as pl
from jax.experimental.pallas import tpu as pltpu

F32 = jnp.float32
BF16 = jnp.bfloat16

D_MODEL = 1024
H_A, KV_A, DH_A = 8, 4, 64
G_A = H_A // KV_A
H_B, KV_B, DH_B = 16, 8, 64
G_B = H_B // KV_B
D_FF = 4 * D_MODEL
N_BUCKETS = 32
MAX_DISTANCE = 128
EPS = 1e-6
NEG = -1e30
LAM_INIT = 0.8 - 0.6 * math.exp(-0.3 * 0)
LOG2E = math.log2(math.e)
Q_SCALE = DH_A ** -0.5 * LOG2E
assert DH_A == DH_B and G_A == 2 and G_B == 2

LANES = 128
VMEM_LIMIT_BYTES = 56 * 1024 * 1024

W_QA, W_KA, W_VA = H_A * 2 * DH_A, KV_A * 2 * DH_A, KV_A * 2 * DH_A
W_QB, W_KB, W_VB = H_B * DH_B, KV_B * DH_B, KV_B * DH_B
_OFFS = np.cumsum([0, W_QA, W_KA, W_VA, W_QB, W_KB, W_VB, D_MODEL, D_MODEL])
N_MAIN = int(_OFFS[-1])

_B_HEAD_PERM = np.array([4 * i + j for i in range(H_B // 4) for j in (0, 2, 1, 3)])
_B_HEAD_INV = np.argsort(_B_HEAD_PERM)


def _cparams(*sem):
    return pltpu.CompilerParams(dimension_semantics=sem, vmem_limit_bytes=VMEM_LIMIT_BYTES)


def _rms(x, g):
    return (x * lax.rsqrt(jnp.mean(x * x, axis=-1, keepdims=True) + EPS)) * g


def _t5_bucket(dist):
    n = jnp.maximum(dist, 0)
    max_exact = N_BUCKETS // 2
    nf = jnp.maximum(n, max_exact).astype(F32)
    large = max_exact + (jnp.log(nf / max_exact) / math.log(MAX_DISTANCE / max_exact)
                         * (N_BUCKETS - max_exact)).astype(jnp.int32)
    large = jnp.minimum(large, N_BUCKETS - 1)
    return jnp.where(n < max_exact, n, large)


def _split3(x):
    hi = x.astype(BF16)
    r1 = x - hi.astype(F32)
    mid = r1.astype(BF16)
    lo = (r1 - mid.astype(F32)).astype(BF16)
    return hi, mid, lo


def _dot(a, b):
    return jnp.dot(a, b, preferred_element_type=F32)


def _dot_nt(a, b):
    return lax.dot_general(a, b, (((1,), (1,)), ((), ())), preferred_element_type=F32)


def _in_proj_body(x_ref, g_ref, w_ref, wfl_ref, bfl_ref,
                  qa_ref, qb_ref, ga_ref, gb_ref, kat_ref, va4_ref, kbt_ref, vbt_ref, lft_ref,
                  ka16_ref, va16_ref, kb16_ref, vb16_ref):
    xb = _rms(x_ref[...], g_ref[...]).astype(BF16)
    tm = xb.shape[0]

    def seg(i):
        return _dot(xb, w_ref[:, int(_OFFS[i]):int(_OFFS[i + 1])])

    qa_ref[...] = (seg(0) * Q_SCALE).astype(BF16)
    for i, ot, o16 in ((1, kat_ref, ka16_ref), (4, kbt_ref, kb16_ref), (5, vbt_ref, vb16_ref)):
        z = seg(i)
        ot[0] = z.T
        o16[...] = z.astype(BF16)
    z = seg(2)
    va16_ref[...] = z.astype(BF16)
    for kv in range(KV_A):
        va4_ref[0, pl.ds(kv, tm, stride=KV_A), :] = z[:, kv * 2 * DH_A:(kv + 1) * 2 * DH_A]
    qb_ref[...] = (seg(3) * Q_SCALE).astype(BF16)
    ga_ref[...] = (1.0 / (1.0 + jnp.exp(-seg(6)))).astype(BF16)
    gb_ref[...] = (1.0 / (1.0 + jnp.exp(-seg(7)))).astype(BF16)
    zf = _dot(xb, wfl_ref[...]).T[:H_B] + bfl_ref[...]
    lft_ref[0] = jnp.minimum(zf, 0.0) - jnp.log1p(jnp.exp(-jnp.abs(zf)))


def _in_proj(x2d, g, w_main, w_fl, b_fl, tm, seq):
    rows, d = x2d.shape
    nb, nt = rows // seq, seq // tm
    row = lambda n: pl.BlockSpec((tm, n), lambda i: (i, 0))
    const = lambda a: pl.BlockSpec(a.shape, lambda i: (0, 0))
    tspec = lambda n: pl.BlockSpec((1, n, tm), lambda i: (i // nt, 0, i % nt))
    widths16 = (W_QA, W_QB, D_MODEL, D_MODEL)
    widths32 = (W_KA, W_VA, W_KB, W_VB)
    out_shape = ([jax.ShapeDtypeStruct((rows, n), BF16) for n in widths16]
                 + [jax.ShapeDtypeStruct((nb, W_KA, seq), F32),
                    jax.ShapeDtypeStruct((nb, seq * KV_A, 2 * DH_A), F32),
                    jax.ShapeDtypeStruct((nb, W_KB, seq), F32),
                    jax.ShapeDtypeStruct((nb, W_VB, seq), F32),
                    jax.ShapeDtypeStruct((nb, H_B, seq), F32)]
                 + [jax.ShapeDtypeStruct((rows, n), BF16) for n in widths32])
    out_specs = ([row(n) for n in widths16]
                 + [tspec(W_KA), pl.BlockSpec((1, tm * KV_A, 2 * DH_A), lambda i: (i // nt, i % nt, 0)),
                    tspec(W_KB), tspec(W_VB), tspec(H_B)]
                 + [row(n) for n in widths32])
    return pl.pallas_call(
        _in_proj_body, grid=(rows // tm,),
        in_specs=[row(d), const(g), const(w_main), const(w_fl), const(b_fl)],
        out_specs=out_specs, out_shape=out_shape,
        compiler_params=_cparams("arbitrary"), name="in_proj")(x2d, g, w_main, w_fl, b_fl)


_CS_BLK = 256


def _cumsum_body(lf_ref, c_ref):
    t = lf_ref.shape[2]
    r = lax.broadcasted_iota(jnp.int32, (_CS_BLK, _CS_BLK), 0)
    c = lax.broadcasted_iota(jnp.int32, (_CS_BLK, _CS_BLK), 1)
    tri = (r <= c).astype(F32).astype(BF16)
    carry = jnp.zeros((H_B, 1), F32)
    for i in range(t // _CS_BLK):
        sl = slice(i * _CS_BLK, (i + 1) * _CS_BLK)
        hi, mid, lo = _split3(lf_ref[0, :, sl])
        cs = _dot(hi, tri) + _dot(mid, tri) + _dot(lo, tri) + carry
        c_ref[0, :, sl] = cs
        carry = cs[:, _CS_BLK - 1:_CS_BLK]


def _cumsum_t(lf_t):
    b, h, t = lf_t.shape
    assert t % _CS_BLK == 0
    spec = pl.BlockSpec((1, h, t), lambda i: (i, 0, 0))
    return pl.pallas_call(
        _cumsum_body, grid=(b,), in_specs=[spec], out_specs=spec,
        out_shape=jax.ShapeDtypeStruct(lf_t.shape, F32),
        compiler_params=_cparams("arbitrary"), name="cumsum")(lf_t)


def _lam(lq1_ref, lk1_ref, lq2_ref, lk2_ref):
    return (jnp.exp(jnp.sum(lq1_ref[...] * lk1_ref[...], keepdims=True))
            - jnp.exp(jnp.sum(lq2_ref[...] * lk2_ref[...], keepdims=True)) + LAM_INIT)


def _lane_chunks(x):
    return [x[:, j * LANES:(j + 1) * LANES] for j in range(x.shape[1] // LANES)]


def _keep_scores(kb, nb, tq, s, s_ref, mpart_ref, first=False):
    for j in range(nb):
        s_ref[kb + j] = s[:, j * tq:(j + 1) * tq]
    chunks = _lane_chunks(s)
    mp = chunks[0] if first else jnp.maximum(mpart_ref[...], chunks[0])
    for sj in chunks[1:]:
        mp = jnp.maximum(mp, sj)
    mpart_ref[...] = mp


def _paired(n, fn):
    def body(i, carry):
        fn(4 * i, 4)
        return carry

    lax.fori_loop(0, n // 4, body, 0)
    for rem in (1, 2, 3):
        @pl.when(n % 4 == rem)
        def _(rem=rem):
            fn((n // 4) * 4, rem)


def _value_pass(last, tq, s_ref, mb_ref, v_ref, acc_ref):
    def blocks(kb, nb, first=False):
        mb = mb_ref[...]
        ps = [jnp.exp2(sj - mb) for j in range(nb) for sj in _lane_chunks(s_ref[kb + j])]
        p16 = jnp.concatenate(ps, axis=1).astype(BF16)
        rows = pl.ds(pl.multiple_of(kb * tq, tq), nb * tq)
        v1 = jnp.concatenate([v_ref[0, rows, :], jnp.ones((nb * tq, LANES), BF16)], axis=1)
        pv = _dot(p16, v1)
        acc_ref[...] = pv if first else acc_ref[...] + pv

    blocks(last, 1, True)
    _paired(last, blocks)
    acc = acc_ref[...]
    return acc[:, :LANES] / acc[:, LANES:]


def _attn_a_body(rb_ref, q_ref, k_ref, v_ref, lq1_ref, lk1_ref, lq2_ref, lk2_ref, sg_ref,
                 o_ref, bias_ref, bucket_ref, q4_ref, s_ref, mpart_ref, mb_ref, acc_ref, *, tq):
    b, kvh, qi = pl.program_id(0), pl.program_id(1), pl.program_id(2)
    far = (N_BUCKETS - 1) * H_A

    @pl.when((b == 0) & (kvh == 0) & (qi == 0))
    def _():
        r = lax.broadcasted_iota(jnp.int32, (tq, tq), 0)
        c = lax.broadcasted_iota(jnp.int32, (tq, tq), 1)
        for t in range(2):
            dist = r - c + t * tq
            bucket_ref[...] = _t5_bucket(dist)
            for h in range(H_A):
                bias_ref[h, t] = jnp.zeros((tq, tq), F32)

                def body(i, carry, h=h, t=t):
                    rel = (rb_ref[i * H_A + h] - rb_ref[far + h]) * LOG2E
                    bias_ref[h, t] = jnp.where(bucket_ref[...] == i, rel, bias_ref[h, t])
                    return carry

                lax.fori_loop(0, N_BUCKETS, body, 0)
                if t == 0:
                    bias_ref[h, t] = jnp.where(dist >= 0, bias_ref[h, t], NEG)

    lane = lax.broadcasted_iota(jnp.int32, (tq, LANES), 1)
    for g in range(G_A):
        qg = q_ref[0, :, g * LANES:(g + 1) * LANES].astype(F32)
        for c in range(2):
            keep = (lane >= DH_A) if c else (lane < DH_A)
            q4_ref[pl.ds((2 * g + c) * tq, tq), :] = jnp.where(keep, qg, 0.0).astype(BF16)

    def score_pass(kb, nb, near=None):
        rows = pl.ds(pl.multiple_of(kb * tq, tq), nb * tq)
        s = _dot_nt(q4_ref[...], k_ref[0, rows, :])
        if near is not None:
            s = jnp.concatenate([s[i * tq:(i + 1) * tq] + bias_ref[kvh * G_A + i // 2, near]
                                 for i in range(4)], axis=0)
        _keep_scores(kb, nb, tq, s, s_ref, mpart_ref, first=(near == 0))

    score_pass(qi, 1, 0)

    @pl.when(qi >= 1)
    def _():
        score_pass(qi - 1, 1, 1)

    _paired(jnp.maximum(qi - 1, 0), score_pass)
    mb_ref[...] = jnp.broadcast_to(jnp.max(mpart_ref[...], axis=-1, keepdims=True), mb_ref.shape)
    o_all = _value_pass(qi, tq, s_ref, mb_ref, v_ref, acc_ref)

    lam = _lam(lq1_ref, lk1_ref, lq2_ref, lk2_ref)
    for g in range(G_A):
        o = o_all[2 * g * tq:(2 * g + 1) * tq] - lam * o_all[(2 * g + 1) * tq:(2 * g + 2) * tq]
        y = _rms(o, sg_ref[...]) * (1.0 - LAM_INIT)
        o_ref[0, :, g * LANES:(g + 1) * LANES] = y.astype(BF16)


def _attn_scratch(tq, nq):
    m = 4 * tq
    return [pltpu.VMEM((m, LANES), BF16), pltpu.VMEM((nq, m, tq), F32), pltpu.VMEM((m, LANES), F32),
            pltpu.VMEM((m, LANES), F32), pltpu.VMEM((m, 2 * LANES), F32)]


def _attn_a(rb_flat, qa, ka16, va16, lq1, lk1, lq2, lk2, sg, tq):
    b, t, _ = qa.shape
    assert t % tq == 0 and tq >= MAX_DISTANCE and tq % LANES == 0
    wq = G_A * 2 * DH_A
    qspec = pl.BlockSpec((1, tq, wq), lambda bi, k, i: (bi, i, k))
    kvspec = pl.BlockSpec((1, t, 2 * DH_A), lambda bi, k, i: (bi, 0, k))
    small = lambda a: pl.BlockSpec(a.shape, lambda bi, k, i: (0, 0))
    return pl.pallas_call(
        functools.partial(_attn_a_body, tq=tq), grid=(b, KV_A, t // tq),
        in_specs=[pl.BlockSpec(memory_space=pltpu.SMEM), qspec, kvspec, kvspec,
                  small(lq1), small(lk1), small(lq2), small(lk2), small(sg)],
        out_specs=qspec, out_shape=jax.ShapeDtypeStruct(qa.shape, BF16),
        scratch_shapes=[pltpu.VMEM((H_A, 2, tq, tq), F32), pltpu.VMEM((tq, tq), jnp.int32)]
        + _attn_scratch(tq, t // tq),
        compiler_params=_cparams("arbitrary", "arbitrary", "arbitrary"), name="attn_a",
    )(rb_flat, qa, ka16, va16, lq1, lk1, lq2, lk2, sg)


def _attn_b_body(q_ref, k_ref, v_ref, cq_ref, ck_ref, o_ref,
                 q4_ref, s_ref, mpart_ref, mb_ref, acc_ref, *, tq):
    qi = pl.program_id(2)
    lane = lax.broadcasted_iota(jnp.int32, (tq, LANES), 1)
    heads = [2 * (i % 2) + i // 2 for i in range(4)]
    for s_ in range(2):
        qs = q_ref[0, :, s_ * LANES:(s_ + 1) * LANES].astype(F32)
        for half in range(2):
            keep = (lane >= DH_B) if half else (lane < DH_B)
            q4_ref[pl.ds((2 * s_ + half) * tq, tq), :] = jnp.where(keep, qs, 0.0).astype(BF16)
    r = lax.broadcasted_iota(jnp.int32, (tq, tq), 0)
    c = lax.broadcasted_iota(jnp.int32, (tq, tq), 1)
    causal = r >= c

    def score_pass(kb, nb, diag=False):
        rows = pl.ds(pl.multiple_of(kb * tq, tq), nb * tq)
        s = _dot_nt(q4_ref[...], k_ref[0, rows, :])
        parts = []
        for i in range(4):
            ck = jnp.concatenate([ck_ref[0, 0, kb + j, heads[i]:heads[i] + 1, :] for j in range(nb)], axis=1)
            si = s[i * tq:(i + 1) * tq] - ck * LOG2E
            parts.append(jnp.where(causal, si, NEG) if diag else si)
        _keep_scores(kb, nb, tq, jnp.concatenate(parts, axis=0), s_ref, mpart_ref, first=diag)

    score_pass(qi, 1, True)
    _paired(qi, score_pass)
    cq = jnp.concatenate([cq_ref[0, 0, :, heads[i]:heads[i] + 1] for i in range(4)], axis=0) * LOG2E
    m = jnp.max(mpart_ref[...], axis=-1, keepdims=True) + cq
    mb_ref[...] = jnp.broadcast_to(m - cq, mb_ref.shape)
    o_all = _value_pass(qi, tq, s_ref, mb_ref, v_ref, acc_ref)
    for s_ in range(2):
        o0 = o_all[2 * s_ * tq:(2 * s_ + 1) * tq]
        o1 = o_all[(2 * s_ + 1) * tq:(2 * s_ + 2) * tq]
        o_ref[0, :, s_ * LANES:(s_ + 1) * LANES] = jnp.where(lane < DH_B, o0, o1).astype(BF16)


def _attn_b(qb, kb16, vb16, cq4, ck4, tq):
    b, t, _ = qb.shape
    npair = KV_B // 2
    qspec = pl.BlockSpec((1, tq, 2 * LANES), lambda bi, k, i: (bi, i, k))
    kvspec = pl.BlockSpec((1, t, LANES), lambda bi, k, i: (bi, 0, k))
    cqspec = pl.BlockSpec((1, 1, tq, 4), lambda bi, k, i: (bi, k, i, 0))
    ckspec = pl.BlockSpec((1, 1, t // tq, 4, tq), lambda bi, k, i: (bi, k, 0, 0, 0))
    return pl.pallas_call(
        functools.partial(_attn_b_body, tq=tq), grid=(b, npair, t // tq),
        in_specs=[qspec, kvspec, kvspec, cqspec, ckspec],
        out_specs=qspec, out_shape=jax.ShapeDtypeStruct(qb.shape, BF16),
        scratch_shapes=_attn_scratch(tq, t // tq),
        compiler_params=_cparams("arbitrary", "arbitrary", "arbitrary"), name="attn_b",
    )(qb, kb16, vb16, cq4, ck4)


def _decode_body(pt_ref, wqa_ref, wqb_ref, kan_ref, van_ref, kbn_ref, vbn_ref, lfn_ref, rbt_ref, *rest,
                 page, pps):
    del pt_ref
    cak, cav, cbk, cbv, clf = (rest[i * pps:(i + 1) * pps] for i in range(5))
    (lq1_ref, lk1_ref, lq2_ref, lk2_ref, sg_ref, oa_ref, ob_ref,
     bias_ref, m_ref, l_ref, acc_ref, carry_ref) = rest[5 * pps:]
    p = pl.program_id(1)
    nrow = 2 * H_A

    @pl.when(p == 0)
    def _():
        sa = jnp.sum(wqa_ref[0].astype(F32) * kan_ref[0], axis=-1, keepdims=True) + rbt_ref[:, 0:1] * LOG2E
        sb = jnp.sum(wqb_ref[0].astype(F32) * kbn_ref[0], axis=-1, keepdims=True)
        m_ref[0] = jnp.broadcast_to(sa, m_ref.shape[1:])
        m_ref[1] = jnp.broadcast_to(sb, m_ref.shape[1:])
        l_ref[...] = jnp.ones(l_ref.shape, F32)
        acc_ref[0] = jnp.broadcast_to(van_ref[0], acc_ref.shape[1:])
        acc_ref[1] = jnp.broadcast_to(vbn_ref[0], acc_ref.shape[1:])
        carry_ref[...] = lfn_ref[0]
        dist = page - lax.broadcasted_iota(jnp.int32, (nrow, page), 1)
        bucket = _t5_bucket(dist)
        acc = jnp.zeros((nrow, page), F32)
        for i in range(N_BUCKETS):
            acc = jnp.where(bucket == i, rbt_ref[:, i:i + 1], acc)
        bias_ref[...] = jnp.broadcast_to(rbt_ref[:, N_BUCKETS - 1:N_BUCKETS] * LOG2E, bias_ref.shape)
        bias_ref[:, 0:page] = acc * LOG2E

    @pl.when(p == 1)
    def _():
        bias_ref[...] = jnp.broadcast_to(rbt_ref[:, N_BUCKETS - 1:N_BUCKETS] * LOG2E, bias_ref.shape)

    def update(idx, s, pv):
        m_prev = m_ref[idx]
        m_new = jnp.maximum(m_prev, jnp.max(s, axis=-1, keepdims=True))
        alpha = jnp.exp2(m_prev - m_new)
        pr = jnp.exp2(s - m_new[:, 0:1])
        l_ref[idx] = alpha * l_ref[idx] + jnp.sum(pr, axis=-1, keepdims=True)
        acc_ref[idx] = alpha[:, 0:1] * acc_ref[idx] + pv(pr.astype(BF16))
        m_ref[idx] = m_new

    kt_a = jnp.concatenate([r[0].astype(BF16) for r in cak], axis=1)
    s_a = _dot(wqa_ref[0], kt_a) + bias_ref[...]
    v_a = jnp.concatenate(
        [jnp.concatenate([r[0, pl.ds(kv, page, stride=KV_A), :] for kv in range(KV_A)], axis=1)
         for r in cav], axis=0).astype(BF16)
    update(0, s_a, lambda pr: _dot(pr, v_a))

    r = lax.broadcasted_iota(jnp.int32, (page, page), 0)
    c = lax.broadcasted_iota(jnp.int32, (page, page), 1)
    later = (r > c).astype(F32).astype(BF16)
    lfs = [ref[0] for ref in clf]
    suf = _dot(jnp.concatenate([piece for lf in lfs for piece in _split3(lf)], axis=0), later)
    carry = carry_ref[...]
    decay = []
    for j, lf in enumerate(lfs):
        o = 3 * H_B * j
        suffix = suf[o:o + H_B] + suf[o + H_B:o + 2 * H_B] + suf[o + 2 * H_B:o + 3 * H_B]
        decay.append(suffix + carry)
        carry = carry + suffix[:, 0:1] + lf[:, 0:1]
    carry_ref[...] = carry
    kt_b = jnp.concatenate([ref[0].astype(BF16) for ref in cbk], axis=1)
    s_b = _dot(wqb_ref[0], kt_b) + jnp.concatenate(decay, axis=1) * LOG2E
    vt_b = jnp.concatenate([ref[0].astype(BF16) for ref in cbv], axis=1)
    update(1, s_b, lambda pr: _dot_nt(pr, vt_b))

    @pl.when(p == pl.num_programs(1) - 1)
    def _():
        rowi = lax.broadcasted_iota(jnp.int32, acc_ref.shape[1:], 0)
        lanei = lax.broadcasted_iota(jnp.int32, acc_ref.shape[1:], 1)

        def own_block(o, keep):
            o = jnp.where(keep, o, 0.0)
            return functools.reduce(lambda a, b: a + b, _lane_chunks(o))

        oa = own_block(acc_ref[0] / l_ref[0][:, 0:1], (lanei // (2 * DH_A)) == ((rowi % H_A) // G_A))
        lam = _lam(lq1_ref, lk1_ref, lq2_ref, lk2_ref)
        o = oa[0:H_A] - lam * oa[H_A:2 * H_A]
        oa_ref[0] = _rms(o, sg_ref[...]) * (1.0 - LAM_INIT)
        ob_ref[0] = own_block(acc_ref[1] / l_ref[1][:, 0:1], (lanei // DH_B) == (rowi // G_B))


def _decode(page_table, wqa, wqb, ka_new, va_new, kb_new, vb_new, lf_new, rb_t,
            cak_t, cav2, cbk_t, cbv_t, clf_t, lq1, lk1, lq2, lk2, sg, pps):
    db, n_pages = page_table.shape
    page = cak_t.shape[2]
    width = cak_t.shape[1]
    assert page >= MAX_DISTANCE and page == LANES and n_pages % pps == 0
    per_b = lambda a: pl.BlockSpec((1,) + a.shape[1:], lambda b, p, pt: (b, 0, 0))
    small = lambda a: pl.BlockSpec(a.shape, lambda b, p, pt: (0, 0))

    def paged(a, j):
        return pl.BlockSpec((1,) + a.shape[1:],
                            lambda b, p, pt: (pt[b * n_pages + n_pages - 1 - (p * pps + j)], 0, 0))

    nrow = 2 * H_A
    caches = (cak_t, cav2, cbk_t, cbv_t, clf_t)
    grid_spec = pltpu.PrefetchScalarGridSpec(
        num_scalar_prefetch=1, grid=(db, n_pages // pps),
        in_specs=[per_b(wqa), per_b(wqb), per_b(ka_new), per_b(va_new), per_b(kb_new), per_b(vb_new),
                  per_b(lf_new), small(rb_t)]
        + [paged(a, j) for a in caches for j in range(pps)]
        + [small(lq1), small(lk1), small(lq2), small(lk2), small(sg)],
        out_specs=[pl.BlockSpec((1, H_A, 2 * DH_A), lambda b, p, pt: (b, 0, 0)),
                   pl.BlockSpec((1, H_B, LANES), lambda b, p, pt: (b, 0, 0))],
        scratch_shapes=[pltpu.VMEM((nrow, pps * page), F32), pltpu.VMEM((2, nrow, LANES), F32),
                        pltpu.VMEM((2, nrow, LANES), F32), pltpu.VMEM((2, nrow, width), F32),
                        pltpu.VMEM((H_B, 1), F32)])
    return pl.pallas_call(
        functools.partial(_decode_body, page=page, pps=pps), grid_spec=grid_spec,
        out_shape=[jax.ShapeDtypeStruct((db, H_A, 2 * DH_A), F32),
                   jax.ShapeDtypeStruct((db, H_B, LANES), F32)],
        compiler_params=_cparams("arbitrary", "arbitrary"), name="decode",
    )(page_table.reshape(-1), wqa, wqb, ka_new, va_new, kb_new, vb_new, lf_new, rb_t,
      *[a for a in caches for _ in range(pps)], lq1, lk1, lq2, lk2, sg)


def _merge_body(x_ref, oa_ref, ob_ref, ga_ref, gb_ref, wpa_ref, wpb_ref, wo_ref, x1_ref):
    pa = _dot(oa_ref[...], wpa_ref[...])
    pb = _dot(ob_ref[...], wpb_ref[...])
    h = ga_ref[...].astype(F32) * pa + gb_ref[...].astype(F32) * pb
    x1_ref[...] = x_ref[...] + _dot(h.astype(BF16), wo_ref[...])


def _merge(x2d, oa, ob, ga, gb, wpa, wpb, wo, tm):
    rows, d = x2d.shape
    row = pl.BlockSpec((tm, d), lambda i: (i, 0))
    const = lambda a: pl.BlockSpec(a.shape, lambda i: (0, 0))
    return pl.pallas_call(
        _merge_body, grid=(rows // tm,),
        in_specs=[row, row, row, row, row, const(wpa), const(wpb), const(wo)],
        out_specs=row, out_shape=jax.ShapeDtypeStruct(x2d.shape, F32),
        compiler_params=_cparams("arbitrary"), name="merge")(x2d, oa, ob, ga, gb, wpa, wpb, wo)


_FF_CHUNK = 1024


def _mlp_body(x_ref, g_ref, wu_ref, wd_ref, gf_ref, y_ref):
    x = x_ref[...]
    hn = _rms(x, g_ref[...]).astype(BF16)
    acc = x
    for c in range(D_FF // _FF_CHUNK):
        sl = slice(c * _FF_CHUNK, (c + 1) * _FF_CHUNK)
        u = jnp.maximum(_dot(hn, wu_ref[:, sl]), 0.0)
        acc = acc + _dot((u * u).astype(BF16), wd_ref[sl, :])
    y_ref[...] = _rms(acc, gf_ref[...])


def _mlp(x2d, g, wu, wd, gf, tm):
    rows, d = x2d.shape
    row = pl.BlockSpec((tm, d), lambda i: (i, 0))
    const = lambda a: pl.BlockSpec(a.shape, lambda i: (0, 0))
    return pl.pallas_call(
        _mlp_body, grid=(rows // tm,),
        in_specs=[row, const(g), const(wu), const(wd), const(gf)],
        out_specs=row, out_shape=jax.ShapeDtypeStruct(x2d.shape, F32),
        compiler_params=_cparams("arbitrary"), name="mlp")(x2d, g, wu, wd, gf)


def _row_tile(rows, want):
    return want if rows % want == 0 else rows


def kernel(x_prompt, x_sample, cache_a_k, cache_a_v, cache_b_k, cache_b_v, cache_b_logf, page_table,
           norm_attn_g, w_in, b_forget, rel_bias, lam_q1, lam_k1, lam_q2, lam_k2, subln_g,
           w_proj_a, w_proj_b, w_out, norm_mlp_g, w_up, w_down, norm_final_g):
    depth = w_in.shape[0]
    assert depth == 1, "one layer supported"
    B, T, D = x_prompt.shape
    DB, TS, _ = x_sample.shape
    assert TS == 1 and D == D_MODEL
    n_pool, page = cache_a_k.shape[1], cache_a_k.shape[2]

    w = w_in[0]
    splits = np.cumsum([W_QA, W_KA, W_VA, W_QB, W_KB, W_VB, H_B, D_MODEL])
    w_qa, w_ka, w_va, w_qb, w_kb, w_vb, w_f, w_ga, w_gb = jnp.split(w, [int(s) for s in splits], axis=1)
    w_qb = w_qb.reshape(D, H_B, DH_B)[:, _B_HEAD_PERM].reshape(D, W_QB)
    w_main = jnp.concatenate([w_qa, w_ka, w_va, w_qb, w_kb, w_vb, w_ga, w_gb], axis=1).astype(BF16)
    w_fl = jnp.pad(w_f, ((0, 0), (0, LANES - H_B))).astype(BF16)
    b_fl = b_forget[0].astype(F32).reshape(H_B, 1)
    g_attn = norm_attn_g[0].reshape(1, D).astype(F32)
    wpa = w_proj_a[0].astype(BF16)
    wpb = w_proj_b[0].reshape(H_B, DH_B, D)[_B_HEAD_PERM].reshape(H_B * DH_B, D).astype(BF16)
    wo = w_out[0].astype(BF16)
    wu = w_up[0].astype(BF16)
    wd = w_down[0].astype(BF16)
    g_mlp = norm_mlp_g[0].reshape(1, D).astype(F32)
    g_fin = norm_final_g.reshape(1, D).astype(F32)
    lq1, lk1, lq2, lk2 = (a[0].reshape(1, DH_A).astype(F32) for a in (lam_q1, lam_k1, lam_q2, lam_k2))
    sg = subln_g[0].reshape(1, 2 * DH_A).astype(F32)
    rb = rel_bias.astype(F32)

    xp = x_prompt.reshape(B * T, D)
    tm = _row_tile(B * T, 256)
    (qa, qb, ga, gb, ka_t, va4, kb_t, vb_t, lf_t, ka16, va16, kb16, vb16) = _in_proj(
        xp, g_attn, w_main, w_fl, b_fl, tm, T)
    c_t = _cumsum_t(lf_t)
    tq = _row_tile(T, 256)
    c4 = c_t.reshape(B, H_B // 4, 4, T)
    cq4 = c4.transpose(0, 1, 3, 2)
    ck4 = c4.reshape(B, H_B // 4, 4, T // tq, tq).transpose(0, 1, 3, 2, 4)
    oa = _attn_a(rb.reshape(-1), qa.reshape(B, T, W_QA), ka16.reshape(B, T, W_KA), va16.reshape(B, T, W_VA),
                 lq1, lk1, lq2, lk2, sg, tq)
    ob = _attn_b(qb.reshape(B, T, W_QB), kb16.reshape(B, T, W_KB), vb16.reshape(B, T, W_VB), cq4, ck4, tq)
    tm2 = _row_tile(B * T, 512)
    x1 = _merge(xp, oa.reshape(B * T, W_QA), ob.reshape(B * T, W_QB), ga, gb, wpa, wpb, wo, tm2)
    y_prompt = _mlp(x1, g_mlp, wu, wd, g_fin, tm2).reshape(B, T, D)

    xs = x_sample.reshape(DB, D)
    tms = _row_tile(DB, 128)
    (qa_s, qb_s, ga_s, gb_s, ka_st, va_s4, kb_st, vb_st, lf_st, _, _, _, _) = _in_proj(
        xs, g_attn, w_main, w_fl, b_fl, tms, DB)
    ka_s, kb_s, vb_s = ka_st[0].T, kb_st[0].T, vb_st[0].T
    va_s = va_s4.reshape(DB, W_VA)
    lf_s = lf_st[0].T
    qa5 = qa_s.reshape(DB, KV_A, G_A, 2, DH_A)
    wqa = jnp.einsum("bkgcd,kK,cC->bckgKCd", qa5, jnp.eye(KV_A, dtype=BF16), jnp.eye(2, dtype=BF16))
    wqa = wqa.reshape(DB, 2 * H_A, W_KA)
    qb4 = qb_s.reshape(DB, H_B, DH_B)[:, _B_HEAD_INV].reshape(DB, KV_B, G_B, DH_B)
    wqb = jnp.einsum("bkgd,kK->bkgKd", qb4, jnp.eye(KV_B, dtype=BF16)).reshape(DB, H_B, W_KB)
    pps = math.gcd(page_table.shape[1], 16)
    rb_t = jnp.tile(rb.T, (2, 1))
    oa_s, ob_raw = _decode(
        page_table, wqa, wqb, ka_s.reshape(DB, 1, W_KA), va_s.reshape(DB, 1, W_VA),
        kb_s.reshape(DB, 1, W_KB), vb_s.reshape(DB, 1, W_VB), lf_s.reshape(DB, H_B, 1), rb_t,
        cache_a_k[0].transpose(0, 2, 3, 4, 1).reshape(n_pool, W_KA, page),
        cache_a_v[0].reshape(n_pool, page * KV_A, 2 * DH_A),
        cache_b_k[0].transpose(0, 2, 3, 1).reshape(n_pool, W_KB, page),
        cache_b_v[0].transpose(0, 2, 3, 1).reshape(n_pool, W_VB, page),
        cache_b_logf[0].transpose(0, 2, 1), lq1, lk1, lq2, lk2, sg, pps)
    odd_kv = ((jnp.arange(H_B) // G_B) % 2 == 1)[None, :, None]
    ob_s = jnp.where(odd_kv, ob_raw[..., DH_B:], ob_raw[..., :DH_B])
    ob_s = ob_s[:, _B_HEAD_PERM].reshape(DB, W_QB).astype(BF16)
    x1_s = _merge(xs, oa_s.reshape(DB, W_QA).astype(BF16), ob_s, ga_s, gb_s, wpa, wpb, wo, tms)
    y_sample = _mlp(x1_s, g_mlp, wu, wd, g_fin, tms).reshape(DB, 1, D)

    return (y_prompt, y_sample,
            ka_t.reshape(1, B, KV_A, 2, DH_A, T).transpose(0, 1, 5, 2, 3, 4),
            va4.reshape(1, B, T, KV_A, 2 * DH_A),
            kb_t.reshape(1, B, KV_B, DH_B, T).transpose(0, 1, 4, 2, 3),
            vb_t.reshape(1, B, KV_B, DH_B, T).transpose(0, 1, 4, 2, 3),
            lf_t.reshape(1, B, H_B, T).transpose(0, 1, 3, 2),
            ka_s.reshape(1, DB, 1, KV_A, 2, DH_A), va_s.reshape(1, DB, 1, KV_A, 2 * DH_A),
            kb_s.reshape(1, DB, 1, KV_B, DH_B), vb_s.reshape(1, DB, 1, KV_B, DH_B),
            lf_s.reshape(1, DB, 1, H_B))
```

```python
import functools
import math

import jax
import jax.numpy as jnp
import numpy as np
from jax import lax
from jax.experimental import pallas as pl
from jax.experimental.pallas import tpu as pltpu

F32 = jnp.float32
BF16 = jnp.bfloat16

D_MODEL = 1024
H_A, KV_A, DH_A = 8, 4, 64
G_A = H_A // KV_A
H_B, KV_B, DH_B = 16, 8, 64
G_B = H_B // KV_B
D_FF = 4 * D_MODEL
N_BUCKETS = 32
MAX_DISTANCE = 128
EPS = 1e-6
NEG = -1e30
LAM_INIT = 0.8 - 0.6 * math.exp(-0.3 * 0)
LOG2E = math.log2(math.e)
Q_SCALE = DH_A ** -0.5 * LOG2E
assert DH_A == DH_B and G_A == 2 and G_B == 2

LANES = 128
VMEM_LIMIT_BYTES = 56 * 1024 * 1024

W_QA, W_KA, W_VA = H_A * 2 * DH_A, KV_A * 2 * DH_A, KV_A * 2 * DH_A
W_QB, W_KB, W_VB = H_B * DH_B, KV_B * DH_B, KV_B * DH_B
_OFFS = np.cumsum([0, W_QA, W_KA, W_VA, W_QB, W_KB, W_VB, D_MODEL, D_MODEL])
N_MAIN = int(_OFFS[-1])

_B_HEAD_PERM = np.array([4 * i + j for i in range(H_B // 4) for j in (0, 2, 1, 3)])
_B_HEAD_INV = np.argsort(_B_HEAD_PERM)


def _cparams(*sem):
    return pltpu.CompilerParams(dimension_semantics=sem, vmem_limit_bytes=VMEM_LIMIT_BYTES)


def _rms(x, g):
    return (x * lax.rsqrt(jnp.mean(x * x, axis=-1, keepdims=True) + EPS)) * g


def _t5_bucket(dist):
    n = jnp.maximum(dist, 0)
    max_exact = N_BUCKETS // 2
    nf = jnp.maximum(n, max_exact).astype(F32)
    large = max_exact + (jnp.log(nf / max_exact) / math.log(MAX_DISTANCE / max_exact)
                         * (N_BUCKETS - max_exact)).astype(jnp.int32)
    large = jnp.minimum(large, N_BUCKETS - 1)
    return jnp.where(n < max_exact, n, large)


def _split3(x):
    hi = x.astype(BF16)
    r1 = x - hi.astype(F32)
    mid = r1.astype(BF16)
    lo = (r1 - mid.astype(F32)).astype(BF16)
    return hi, mid, lo


def _dot(a, b):
    return jnp.dot(a, b, preferred_element_type=F32)


def _dot_nt(a, b):
    return lax.dot_general(a, b, (((1,), (1,)), ((), ())), preferred_element_type=F32)


def _in_proj_body(x_ref, g_ref, w_ref, wfl_ref, bfl_ref,
                  qa_ref, qb_ref, ga_ref, gb_ref, kat_ref, va4_ref, kbt_ref, vbt_ref, lft_ref,
                  ka16_ref, va16_ref, kb16_ref, vb16_ref):
    xb = _rms(x_ref[...], g_ref[...]).astype(BF16)
    tm = xb.shape[0]

    def seg(i):
        return _dot(xb, w_ref[:, int(_OFFS[i]):int(_OFFS[i + 1])])

    qa_ref[...] = (seg(0) * Q_SCALE).astype(BF16)
    for i, ot, o16 in ((1, kat_ref, ka16_ref), (4, kbt_ref, kb16_ref), (5, vbt_ref, vb16_ref)):
        z = seg(i)
        ot[0] = z.T
        o16[...] = z.astype(BF16)
    z = seg(2)
    va16_ref[...] = z.astype(BF16)
    for kv in range(KV_A):
        va4_ref[0, pl.ds(kv, tm, stride=KV_A), :] = z[:, kv * 2 * DH_A:(kv + 1) * 2 * DH_A]
    qb_ref[...] = (seg(3) * Q_SCALE).astype(BF16)
    ga_ref[...] = (1.0 / (1.0 + jnp.exp(-seg(6)))).astype(BF16)
    gb_ref[...] = (1.0 / (1.0 + jnp.exp(-seg(7)))).astype(BF16)
    zf = _dot(xb, wfl_ref[...]).T[:H_B] + bfl_ref[...]
    lft_ref[0] = jnp.minimum(zf, 0.0) - jnp.log1p(jnp.exp(-jnp.abs(zf)))


def _in_proj(x2d, g, w_main, w_fl, b_fl, tm, seq):
    rows, d = x2d.shape
    nb, nt = rows // seq, seq // tm
    row = lambda n: pl.BlockSpec((tm, n), lambda i: (i, 0))
    const = lambda a: pl.BlockSpec(a.shape, lambda i: (0, 0))
    tspec = lambda n: pl.BlockSpec((1, n, tm), lambda i: (i // nt, 0, i % nt))
    widths16 = (W_QA, W_QB, D_MODEL, D_MODEL)
    widths32 = (W_KA, W_VA, W_KB, W_VB)
    out_shape = ([jax.ShapeDtypeStruct((rows, n), BF16) for n in widths16]
                 + [jax.ShapeDtypeStruct((nb, W_KA, seq), F32),
                    jax.ShapeDtypeStruct((nb, seq * KV_A, 2 * DH_A), F32),
                    jax.ShapeDtypeStruct((nb, W_KB, seq), F32),
                    jax.ShapeDtypeStruct((nb, W_VB, seq), F32),
                    jax.ShapeDtypeStruct((nb, H_B, seq), F32)]
                 + [jax.ShapeDtypeStruct((rows, n), BF16) for n in widths32])
    out_specs = ([row(n) for n in widths16]
                 + [tspec(W_KA), pl.BlockSpec((1, tm * KV_A, 2 * DH_A), lambda i: (i // nt, i % nt, 0)),
                    tspec(W_KB), tspec(W_VB), tspec(H_B)]
                 + [row(n) for n in widths32])
    return pl.pallas_call(
        _in_proj_body, grid=(rows // tm,),
        in_specs=[row(d), const(g), const(w_main), const(w_fl), const(b_fl)],
        out_specs=out_specs, out_shape=out_shape,
        compiler_params=_cparams("arbitrary"), name="in_proj")(x2d, g, w_main, w_fl, b_fl)


_CS_BLK = 256


def _cumsum_body(lf_ref, c_ref):
    t = lf_ref.shape[2]
    r = lax.broadcasted_iota(jnp.int32, (_CS_BLK, _CS_BLK), 0)
    c = lax.broadcasted_iota(jnp.int32, (_CS_BLK, _CS_BLK), 1)
    tri = (r <= c).astype(F32).astype(BF16)
    carry = jnp.zeros((H_B, 1), F32)
    for i in range(t // _CS_BLK):
        sl = slice(i * _CS_BLK, (i + 1) * _CS_BLK)
        hi, mid, lo = _split3(lf_ref[0, :, sl])
        cs = _dot(hi, tri) + _dot(mid, tri) + _dot(lo, tri) + carry
        c_ref[0, :, sl] = cs
        carry = cs[:, _CS_BLK - 1:_CS_BLK]


def _cumsum_t(lf_t):
    b, h, t = lf_t.shape
    assert t % _CS_BLK == 0
    spec = pl.BlockSpec((1, h, t), lambda i: (i, 0, 0))
    return pl.pallas_call(
        _cumsum_body, grid=(b,), in_specs=[spec], out_specs=spec,
        out_shape=jax.ShapeDtypeStruct(lf_t.shape, F32),
        compiler_params=_cparams("arbitrary"), name="cumsum")(lf_t)


def _lam(lq1_ref, lk1_ref, lq2_ref, lk2_ref):
    return (jnp.exp(jnp.sum(lq1_ref[...] * lk1_ref[...], keepdims=True))
            - jnp.exp(jnp.sum(lq2_ref[...] * lk2_ref[...], keepdims=True)) + LAM_INIT)


def _lane_chunks(x):
    return [x[:, j * LANES:(j + 1) * LANES] for j in range(x.shape[1] // LANES)]


def _keep_scores(kb, nb, tq, s, s_ref, mpart_ref, first=False):
    for j in range(nb):
        s_ref[kb + j] = s[:, j * tq:(j + 1) * tq]
    chunks = _lane_chunks(s)
    mp = chunks[0] if first else jnp.maximum(mpart_ref[...], chunks[0])
    for sj in chunks[1:]:
        mp = jnp.maximum(mp, sj)
    mpart_ref[...] = mp


def _paired(n, fn):
    def body(i, carry):
        fn(4 * i, 4)
        return carry

    lax.fori_loop(0, n // 4, body, 0)
    for rem in (1, 2, 3):
        @pl.when(n % 4 == rem)
        def _(rem=rem):
            fn((n // 4) * 4, rem)


def _value_pass(last, tq, s_ref, mb_ref, v_ref, acc_ref):
    def blocks(kb, nb, first=False):
        mb = mb_ref[...]
        ps = [jnp.exp2(sj - mb) for j in range(nb) for sj in _lane_chunks(s_ref[kb + j])]
        p16 = jnp.concatenate(ps, axis=1).astype(BF16)
        rows = pl.ds(pl.multiple_of(kb * tq, tq), nb * tq)
        v1 = jnp.concatenate([v_ref[0, rows, :], jnp.ones((nb * tq, LANES), BF16)], axis=1)
        pv = _dot(p16, v1)
        acc_ref[...] = pv if first else acc_ref[...] + pv

    @pl.when(last == 0)
    def _():
        blocks(0, 1, True)

    @pl.when(last >= 1)
    def _():
        blocks(last - 1, 2, True)

    _paired(jnp.maximum(last - 1, 0), blocks)
    acc = acc_ref[...]
    return acc[:, :LANES] / acc[:, LANES:]


def _attn_a_body(rb_ref, q_ref, k_ref, v_ref, lq1_ref, lk1_ref, lq2_ref, lk2_ref, sg_ref,
                 o_ref, bias_ref, bucket_ref, q4_ref, s_ref, mpart_ref, mb_ref, acc_ref, *, tq):
    b, kvh, qi = pl.program_id(0), pl.program_id(1), pl.program_id(2)
    far = (N_BUCKETS - 1) * H_A

    @pl.when((b == 0) & (kvh == 0) & (qi == 0))
    def _():
        r = lax.broadcasted_iota(jnp.int32, (tq, tq), 0)
        c = lax.broadcasted_iota(jnp.int32, (tq, tq), 1)
        for t in range(2):
            dist = r - c + t * tq
            bucket_ref[...] = _t5_bucket(dist)
            for h in range(H_A):
                bias_ref[h, t] = jnp.zeros((tq, tq), F32)

                def body(i, carry, h=h, t=t):
                    rel = (rb_ref[i * H_A + h] - rb_ref[far + h]) * LOG2E
                    bias_ref[h, t] = jnp.where(bucket_ref[...] == i, rel, bias_ref[h, t])
                    return carry

                lax.fori_loop(0, N_BUCKETS, body, 0)
                if t == 0:
                    bias_ref[h, t] = jnp.where(dist >= 0, bias_ref[h, t], NEG)

    lane = lax.broadcasted_iota(jnp.int32, (tq, LANES), 1)
    for g in range(G_A):
        qg = q_ref[0, :, g * LANES:(g + 1) * LANES].astype(F32)
        for c in range(2):
            keep = (lane >= DH_A) if c else (lane < DH_A)
            q4_ref[pl.ds((2 * g + c) * tq, tq), :] = jnp.where(keep, qg, 0.0).astype(BF16)

    def score_pass(kb, nb, near=None):
        rows = pl.ds(pl.multiple_of(kb * tq, tq), nb * tq)
        s = _dot_nt(q4_ref[...], k_ref[0, rows, :])
        if near is not None:
            parts = []
            for i in range(4):
                h = kvh * G_A + i // 2
                parts.append(s[i * tq:(i + 1) * tq] + jnp.concatenate([bias_ref[h, t] for t in near], axis=1))
            s = jnp.concatenate(parts, axis=0)
        _keep_scores(kb, nb, tq, s, s_ref, mpart_ref, first=near is not None)

    @pl.when(qi == 0)
    def _():
        score_pass(0, 1, (0,))

    @pl.when(qi >= 1)
    def _():
        score_pass(qi - 1, 2, (1, 0))

    _paired(jnp.maximum(qi - 1, 0), score_pass)
    mb_ref[...] = jnp.broadcast_to(jnp.max(mpart_ref[...], axis=-1, keepdims=True), mb_ref.shape)
    o_all = _value_pass(qi, tq, s_ref, mb_ref, v_ref, acc_ref)

    lam = _lam(lq1_ref, lk1_ref, lq2_ref, lk2_ref)
    for g in range(G_A):
        o = o_all[2 * g * tq:(2 * g + 1) * tq] - lam * o_all[(2 * g + 1) * tq:(2 * g + 2) * tq]
        y = _rms(o, sg_ref[...]) * (1.0 - LAM_INIT)
        o_ref[0, :, g * LANES:(g + 1) * LANES] = y.astype(BF16)


def _attn_scratch(tq, nq):
    m = 4 * tq
    return [pltpu.VMEM((m, LANES), BF16), pltpu.VMEM((nq, m, tq), F32), pltpu.VMEM((m, LANES), F32),
            pltpu.VMEM((m, LANES), F32), pltpu.VMEM((m, 2 * LANES), F32)]


def _attn_a(rb_flat, qa, ka16, va16, lq1, lk1, lq2, lk2, sg, tq):
    b, t, _ = qa.shape
    assert t % tq == 0 and tq >= MAX_DISTANCE and tq % LANES == 0
    wq = G_A * 2 * DH_A
    qspec = pl.BlockSpec((1, tq, wq), lambda bi, k, i: (bi, i, k))
    kvspec = pl.BlockSpec((1, t, 2 * DH_A), lambda bi, k, i: (bi, 0, k))
    small = lambda a: pl.BlockSpec(a.shape, lambda bi, k, i: (0, 0))
    return pl.pallas_call(
        functools.partial(_attn_a_body, tq=tq), grid=(b, KV_A, t // tq),
        in_specs=[pl.BlockSpec(memory_space=pltpu.SMEM), qspec, kvspec, kvspec,
                  small(lq1), small(lk1), small(lq2), small(lk2), small(sg)],
        out_specs=qspec, out_shape=jax.ShapeDtypeStruct(qa.shape, BF16),
        scratch_shapes=[pltpu.VMEM((H_A, 2, tq, tq), F32), pltpu.VMEM((tq, tq), jnp.int32)]
        + _attn_scratch(tq, t // tq),
        compiler_params=_cparams("arbitrary", "arbitrary", "arbitrary"), name="attn_a",
    )(rb_flat, qa, ka16, va16, lq1, lk1, lq2, lk2, sg)


def _attn_b_body(q_ref, k_ref, v_ref, cq_ref, ck_ref, o_ref,
                 q4_ref, s_ref, mpart_ref, mb_ref, acc_ref, *, tq):
    qi = pl.program_id(2)
    lane = lax.broadcasted_iota(jnp.int32, (tq, LANES), 1)
    heads = [2 * (i % 2) + i // 2 for i in range(4)]
    for s_ in range(2):
        qs = q_ref[0, :, s_ * LANES:(s_ + 1) * LANES].astype(F32)
        for half in range(2):
            keep = (lane >= DH_B) if half else (lane < DH_B)
            q4_ref[pl.ds((2 * s_ + half) * tq, tq), :] = jnp.where(keep, qs, 0.0).astype(BF16)
    r = lax.broadcasted_iota(jnp.int32, (tq, tq), 0)
    c = lax.broadcasted_iota(jnp.int32, (tq, tq), 1)
    causal = r >= c

    def score_pass(kb, nb, diag=False):
        rows = pl.ds(pl.multiple_of(kb * tq, tq), nb * tq)
        s = _dot_nt(q4_ref[...], k_ref[0, rows, :])
        keep = jnp.concatenate([jnp.ones((tq, (nb - 1) * tq), jnp.bool_), causal], axis=1) if nb > 1 else causal
        parts = []
        for i in range(4):
            ck = jnp.concatenate([ck_ref[0, 0, kb + j, heads[i]:heads[i] + 1, :] for j in range(nb)], axis=1)
            si = s[i * tq:(i + 1) * tq] - ck * LOG2E
            parts.append(jnp.where(keep, si, NEG) if diag else si)
        _keep_scores(kb, nb, tq, jnp.concatenate(parts, axis=0), s_ref, mpart_ref, first=diag)

    @pl.when(qi == 0)
    def _():
        score_pass(0, 1, True)

    @pl.when(qi >= 1)
    def _():
        score_pass(qi - 1, 2, True)

    _paired(jnp.maximum(qi - 1, 0), score_pass)
    cq = jnp.concatenate([cq_ref[0, 0, :, heads[i]:heads[i] + 1] for i in range(4)], axis=0) * LOG2E
    m = jnp.max(mpart_ref[...], axis=-1, keepdims=True) + cq
    mb_ref[...] = jnp.broadcast_to(m - cq, mb_ref.shape)
    o_all = _value_pass(qi, tq, s_ref, mb_ref, v_ref, acc_ref)
    for s_ in range(2):
        o0 = o_all[2 * s_ * tq:(2 * s_ + 1) * tq]
        o1 = o_all[(2 * s_ + 1) * tq:(2 * s_ + 2) * tq]
        o_ref[0, :, s_ * LANES:(s_ + 1) * LANES] = jnp.where(lane < DH_B, o0, o1).astype(BF16)


def _attn_b(qb, kb16, vb16, cq4, ck4, tq):
    b, t, _ = qb.shape
    npair = KV_B // 2
    qspec = pl.BlockSpec((1, tq, 2 * LANES), lambda bi, k, i: (bi, i, k))
    kvspec = pl.BlockSpec((1, t, LANES), lambda bi, k, i: (bi, 0, k))
    cqspec = pl.BlockSpec((1, 1, tq, 4), lambda bi, k, i: (bi, k, i, 0))
    ckspec = pl.BlockSpec((1, 1, t // tq, 4, tq), lambda bi, k, i: (bi, k, 0, 0, 0))
    return pl.pallas_call(
        functools.partial(_attn_b_body, tq=tq), grid=(b, npair, t // tq),
        in_specs=[qspec, kvspec, kvspec, cqspec, ckspec],
        out_specs=qspec, out_shape=jax.ShapeDtypeStruct(qb.shape, BF16),
        scratch_shapes=_attn_scratch(tq, t // tq),
        compiler_params=_cparams("arbitrary", "arbitrary", "arbitrary"), name="attn_b",
    )(qb, kb16, vb16, cq4, ck4)


def _decode_body(pt_ref, wqa_ref, wqb_ref, kan_ref, van_ref, kbn_ref, vbn_ref, lfn_ref, rbt_ref, *rest,
                 page, pps):
    del pt_ref
    cak, cav, cbk, cbv, clf = (rest[i * pps:(i + 1) * pps] for i in range(5))
    (lq1_ref, lk1_ref, lq2_ref, lk2_ref, sg_ref, oa_ref, ob_ref,
     bias_ref, m_ref, l_ref, acc_ref, carry_ref) = rest[5 * pps:]
    p = pl.program_id(1)
    nrow = 2 * H_A

    @pl.when(p == 0)
    def _():
        sa = jnp.sum(wqa_ref[0].astype(F32) * kan_ref[0], axis=-1, keepdims=True) + rbt_ref[:, 0:1] * LOG2E
        sb = jnp.sum(wqb_ref[0].astype(F32) * kbn_ref[0], axis=-1, keepdims=True)
        m_ref[0] = jnp.broadcast_to(sa, m_ref.shape[1:])
        m_ref[1] = jnp.broadcast_to(sb, m_ref.shape[1:])
        l_ref[...] = jnp.ones(l_ref.shape, F32)
        acc_ref[0] = jnp.broadcast_to(van_ref[0], acc_ref.shape[1:])
        acc_ref[1] = jnp.broadcast_to(vbn_ref[0], acc_ref.shape[1:])
        carry_ref[...] = lfn_ref[0]
        dist = page - lax.broadcasted_iota(jnp.int32, (nrow, page), 1)
        bucket = _t5_bucket(dist)
        acc = jnp.zeros((nrow, page), F32)
        for i in range(N_BUCKETS):
            acc = jnp.where(bucket == i, rbt_ref[:, i:i + 1], acc)
        bias_ref[...] = jnp.broadcast_to(rbt_ref[:, N_BUCKETS - 1:N_BUCKETS] * LOG2E, bias_ref.shape)
        bias_ref[:, 0:page] = acc * LOG2E

    @pl.when(p == 1)
    def _():
        bias_ref[...] = jnp.broadcast_to(rbt_ref[:, N_BUCKETS - 1:N_BUCKETS] * LOG2E, bias_ref.shape)

    def update(idx, s, pv):
        m_prev = m_ref[idx]
        m_new = jnp.maximum(m_prev, jnp.max(s, axis=-1, keepdims=True))
        alpha = jnp.exp2(m_prev - m_new)
        pr = jnp.exp2(s - m_new[:, 0:1])
        l_ref[idx] = alpha * l_ref[idx] + jnp.sum(pr, axis=-1, keepdims=True)
        acc_ref[idx] = alpha[:, 0:1] * acc_ref[idx] + pv(pr.astype(BF16))
        m_ref[idx] = m_new

    kt_a = jnp.concatenate([r[0].astype(BF16) for r in cak], axis=1)
    s_a = _dot(wqa_ref[0], kt_a) + bias_ref[...]
    v_a = jnp.concatenate(
        [jnp.concatenate([r[0, pl.ds(kv, page, stride=KV_A), :] for kv in range(KV_A)], axis=1)
         for r in cav], axis=0).astype(BF16)
    update(0, s_a, lambda pr: _dot(pr, v_a))

    r = lax.broadcasted_iota(jnp.int32, (page, page), 0)
    c = lax.broadcasted_iota(jnp.int32, (page, page), 1)
    later = (r > c).astype(F32).astype(BF16)
    lfs = [ref[0] for ref in clf]
    suf = _dot(jnp.concatenate([piece for lf in lfs for piece in _split3(lf)], axis=0), later)
    carry = carry_ref[...]
    decay = []
    for j, lf in enumerate(lfs):
        o = 3 * H_B * j
        suffix = suf[o:o + H_B] + suf[o + H_B:o + 2 * H_B] + suf[o + 2 * H_B:o + 3 * H_B]
        decay.append(suffix + carry)
        carry = carry + suffix[:, 0:1] + lf[:, 0:1]
    carry_ref[...] = carry
    kt_b = jnp.concatenate([ref[0].astype(BF16) for ref in cbk], axis=1)
    s_b = _dot(wqb_ref[0], kt_b) + jnp.concatenate(decay, axis=1) * LOG2E
    vt_b = jnp.concatenate([ref[0].astype(BF16) for ref in cbv], axis=1)
    update(1, s_b, lambda pr: _dot_nt(pr, vt_b))

    @pl.when(p == pl.num_programs(1) - 1)
    def _():
        rowi = lax.broadcasted_iota(jnp.int32, acc_ref.shape[1:], 0)
        lanei = lax.broadcasted_iota(jnp.int32, acc_ref.shape[1:], 1)

        def own_block(o, keep):
            o = jnp.where(keep, o, 0.0)
            return functools.reduce(lambda a, b: a + b, _lane_chunks(o))

        oa = own_block(acc_ref[0] / l_ref[0][:, 0:1], (lanei // (2 * DH_A)) == ((rowi % H_A) // G_A))
        lam = _lam(lq1_ref, lk1_ref, lq2_ref, lk2_ref)
        o = oa[0:H_A] - lam * oa[H_A:2 * H_A]
        oa_ref[0] = _rms(o, sg_ref[...]) * (1.0 - LAM_INIT)
        ob_ref[0] = own_block(acc_ref[1] / l_ref[1][:, 0:1], (lanei // DH_B) == (rowi // G_B))


def _decode(page_table, wqa, wqb, ka_new, va_new, kb_new, vb_new, lf_new, rb_t,
            cak_t, cav2, cbk_t, cbv_t, clf_t, lq1, lk1, lq2, lk2, sg, pps):
    db, n_pages = page_table.shape
    page = cak_t.shape[2]
    width = cak_t.shape[1]
    assert page >= MAX_DISTANCE and page == LANES and n_pages % pps == 0
    per_b = lambda a: pl.BlockSpec((1,) + a.shape[1:], lambda b, p, pt: (b, 0, 0))
    small = lambda a: pl.BlockSpec(a.shape, lambda b, p, pt: (0, 0))

    def paged(a, j):
        return pl.BlockSpec((1,) + a.shape[1:],
                            lambda b, p, pt: (pt[b * n_pages + n_pages - 1 - (p * pps + j)], 0, 0))

    nrow = 2 * H_A
    caches = (cak_t, cav2, cbk_t, cbv_t, clf_t)
    grid_spec = pltpu.PrefetchScalarGridSpec(
        num_scalar_prefetch=1, grid=(db, n_pages // pps),
        in_specs=[per_b(wqa), per_b(wqb), per_b(ka_new), per_b(va_new), per_b(kb_new), per_b(vb_new),
                  per_b(lf_new), small(rb_t)]
        + [paged(a, j) for a in caches for j in range(pps)]
        + [small(lq1), small(lk1), small(lq2), small(lk2), small(sg)],
        out_specs=[pl.BlockSpec((1, H_A, 2 * DH_A), lambda b, p, pt: (b, 0, 0)),
                   pl.BlockSpec((1, H_B, LANES), lambda b, p, pt: (b, 0, 0))],
        scratch_shapes=[pltpu.VMEM((nrow, pps * page), F32), pltpu.VMEM((2, nrow, LANES), F32),
                        pltpu.VMEM((2, nrow, LANES), F32), pltpu.VMEM((2, nrow, width), F32),
                        pltpu.VMEM((H_B, 1), F32)])
    return pl.pallas_call(
        functools.partial(_decode_body, page=page, pps=pps), grid_spec=grid_spec,
        out_shape=[jax.ShapeDtypeStruct((db, H_A, 2 * DH_A), F32),
                   jax.ShapeDtypeStruct((db, H_B, LANES), F32)],
        compiler_params=_cparams("arbitrary", "arbitrary"), name="decode",
    )(page_table.reshape(-1), wqa, wqb, ka_new, va_new, kb_new, vb_new, lf_new, rb_t,
      *[a for a in caches for _ in range(pps)], lq1, lk1, lq2, lk2, sg)


def _merge_body(x_ref, oa_ref, ob_ref, ga_ref, gb_ref, wpa_ref, wpb_ref, wo_ref, x1_ref):
    pa = _dot(oa_ref[...], wpa_ref[...])
    pb = _dot(ob_ref[...], wpb_ref[...])
    h = ga_ref[...].astype(F32) * pa + gb_ref[...].astype(F32) * pb
    x1_ref[...] = x_ref[...] + _dot(h.astype(BF16), wo_ref[...])


def _merge(x2d, oa, ob, ga, gb, wpa, wpb, wo, tm):
    rows, d = x2d.shape
    row = pl.BlockSpec((tm, d), lambda i: (i, 0))
    const = lambda a: pl.BlockSpec(a.shape, lambda i: (0, 0))
    return pl.pallas_call(
        _merge_body, grid=(rows // tm,),
        in_specs=[row, row, row, row, row, const(wpa), const(wpb), const(wo)],
        out_specs=row, out_shape=jax.ShapeDtypeStruct(x2d.shape, F32),
        compiler_params=_cparams("arbitrary"), name="merge")(x2d, oa, ob, ga, gb, wpa, wpb, wo)


_FF_CHUNK = 1024


def _mlp_body(x_ref, g_ref, wu_ref, wd_ref, gf_ref, y_ref):
    x = x_ref[...]
    hn = _rms(x, g_ref[...]).astype(BF16)
    acc = x
    for c in range(D_FF // _FF_CHUNK):
        sl = slice(c * _FF_CHUNK, (c + 1) * _FF_CHUNK)
        u = jnp.maximum(_dot(hn, wu_ref[:, sl]), 0.0)
        acc = acc + _dot((u * u).astype(BF16), wd_ref[sl, :])
    y_ref[...] = _rms(acc, gf_ref[...])


def _mlp(x2d, g, wu, wd, gf, tm):
    rows, d = x2d.shape
    row = pl.BlockSpec((tm, d), lambda i: (i, 0))
    const = lambda a: pl.BlockSpec(a.shape, lambda i: (0, 0))
    return pl.pallas_call(
        _mlp_body, grid=(rows // tm,),
        in_specs=[row, const(g), const(wu), const(wd), const(gf)],
        out_specs=row, out_shape=jax.ShapeDtypeStruct(x2d.shape, F32),
        compiler_params=_cparams("arbitrary"), name="mlp")(x2d, g, wu, wd, gf)


def _row_tile(rows, want):
    return want if rows % want == 0 else rows


def kernel(x_prompt, x_sample, cache_a_k, cache_a_v, cache_b_k, cache_b_v, cache_b_logf, page_table,
           norm_attn_g, w_in, b_forget, rel_bias, lam_q1, lam_k1, lam_q2, lam_k2, subln_g,
           w_proj_a, w_proj_b, w_out, norm_mlp_g, w_up, w_down, norm_final_g):
    depth = w_in.shape[0]
    assert depth == 1, "one layer supported"
    B, T, D = x_prompt.shape
    DB, TS, _ = x_sample.shape
    assert TS == 1 and D == D_MODEL
    n_pool, page = cache_a_k.shape[1], cache_a_k.shape[2]

    w = w_in[0]
    splits = np.cumsum([W_QA, W_KA, W_VA, W_QB, W_KB, W_VB, H_B, D_MODEL])
    w_qa, w_ka, w_va, w_qb, w_kb, w_vb, w_f, w_ga, w_gb = jnp.split(w, [int(s) for s in splits], axis=1)
    w_qb = w_qb.reshape(D, H_B, DH_B)[:, _B_HEAD_PERM].reshape(D, W_QB)
    w_main = jnp.concatenate([w_qa, w_ka, w_va, w_qb, w_kb, w_vb, w_ga, w_gb], axis=1).astype(BF16)
    w_fl = jnp.pad(w_f, ((0, 0), (0, LANES - H_B))).astype(BF16)
    b_fl = b_forget[0].astype(F32).reshape(H_B, 1)
    g_attn = norm_attn_g[0].reshape(1, D).astype(F32)
    wpa = w_proj_a[0].astype(BF16)
    wpb = w_proj_b[0].reshape(H_B, DH_B, D)[_B_HEAD_PERM].reshape(H_B * DH_B, D).astype(BF16)
    wo = w_out[0].astype(BF16)
    wu = w_up[0].astype(BF16)
    wd = w_down[0].astype(BF16)
    g_mlp = norm_mlp_g[0].reshape(1, D).astype(F32)
    g_fin = norm_final_g.reshape(1, D).astype(F32)
    lq1, lk1, lq2, lk2 = (a[0].reshape(1, DH_A).astype(F32) for a in (lam_q1, lam_k1, lam_q2, lam_k2))
    sg = subln_g[0].reshape(1, 2 * DH_A).astype(F32)
    rb = rel_bias.astype(F32)

    xp = x_prompt.reshape(B * T, D)
    tm = _row_tile(B * T, 256)
    (qa, qb, ga, gb, ka_t, va4, kb_t, vb_t, lf_t, ka16, va16, kb16, vb16) = _in_proj(
        xp, g_attn, w_main, w_fl, b_fl, tm, T)
    c_t = _cumsum_t(lf_t)
    tq = _row_tile(T, 256)
    c4 = c_t.reshape(B, H_B // 4, 4, T)
    cq4 = c4.transpose(0, 1, 3, 2)
    ck4 = c4.reshape(B, H_B // 4, 4, T // tq, tq).transpose(0, 1, 3, 2, 4)
    oa = _attn_a(rb.reshape(-1), qa.reshape(B, T, W_QA), ka16.reshape(B, T, W_KA), va16.reshape(B, T, W_VA),
                 lq1, lk1, lq2, lk2, sg, tq)
    ob = _attn_b(qb.reshape(B, T, W_QB), kb16.reshape(B, T, W_KB), vb16.reshape(B, T, W_VB), cq4, ck4, tq)
    tm2 = _row_tile(B * T, 512)
    x1 = _merge(xp, oa.reshape(B * T, W_QA), ob.reshape(B * T, W_QB), ga, gb, wpa, wpb, wo, tm2)
    y_prompt = _mlp(x1, g_mlp, wu, wd, g_fin, tm2).reshape(B, T, D)

    xs = x_sample.reshape(DB, D)
    tms = _row_tile(DB, 128)
    (qa_s, qb_s, ga_s, gb_s, ka_st, va_s4, kb_st, vb_st, lf_st, _, _, _, _) = _in_proj(
        xs, g_attn, w_main, w_fl, b_fl, tms, DB)
    ka_s, kb_s, vb_s = ka_st[0].T, kb_st[0].T, vb_st[0].T
    va_s = va_s4.reshape(DB, W_VA)
    lf_s = lf_st[0].T
    qa5 = qa_s.reshape(DB, KV_A, G_A, 2, DH_A)
    wqa = jnp.einsum("bkgcd,kK,cC->bckgKCd", qa5, jnp.eye(KV_A, dtype=BF16), jnp.eye(2, dtype=BF16))
    wqa = wqa.reshape(DB, 2 * H_A, W_KA)
    qb4 = qb_s.reshape(DB, H_B, DH_B)[:, _B_HEAD_INV].reshape(DB, KV_B, G_B, DH_B)
    wqb = jnp.einsum("bkgd,kK->bkgKd", qb4, jnp.eye(KV_B, dtype=BF16)).reshape(DB, H_B, W_KB)
    pps = math.gcd(page_table.shape[1], 16)
    rb_t = jnp.tile(rb.T, (2, 1))
    oa_s, ob_raw = _decode(
        page_table, wqa, wqb, ka_s.reshape(DB, 1, W_KA), va_s.reshape(DB, 1, W_VA),
        kb_s.reshape(DB, 1, W_KB), vb_s.reshape(DB, 1, W_VB), lf_s.reshape(DB, H_B, 1), rb_t,
        cache_a_k[0].transpose(0, 2, 3, 4, 1).reshape(n_pool, W_KA, page),
        cache_a_v[0].reshape(n_pool, page * KV_A, 2 * DH_A),
        cache_b_k[0].transpose(0, 2, 3, 1).reshape(n_pool, W_KB, page),
        cache_b_v[0].transpose(0, 2, 3, 1).reshape(n_pool, W_VB, page),
        cache_b_logf[0].transpose(0, 2, 1), lq1, lk1, lq2, lk2, sg, pps)
    odd_kv = ((jnp.arange(H_B) // G_B) % 2 == 1)[None, :, None]
    ob_s = jnp.where(odd_kv, ob_raw[..., DH_B:], ob_raw[..., :DH_B])
    ob_s = ob_s[:, _B_HEAD_PERM].reshape(DB, W_QB).astype(BF16)
    x1_s = _merge(xs, oa_s.reshape(DB, W_QA).astype(BF16), ob_s, ga_s, gb_s, wpa, wpb, wo, tms)
    y_sample = _mlp(x1_s, g_mlp, wu, wd, g_fin, tms).reshape(DB, 1, D)

    return (y_prompt, y_sample,
            ka_t.reshape(1, B, KV_A, 2, DH_A, T).transpose(0, 1, 5, 2, 3, 4),
            va4.reshape(1, B, T, KV_A, 2 * DH_A),
            kb_t.reshape(1, B, KV_B, DH_B, T).transpose(0, 1, 4, 2, 3),
            vb_t.reshape(1, B, KV_B, DH_B, T).transpose(0, 1, 4, 2, 3),
            lf_t.reshape(1, B, H_B, T).transpose(0, 1, 3, 2),
            ka_s.reshape(1, DB, 1, KV_A, 2, DH_A), va_s.reshape(1, DB, 1, KV_A, 2 * DH_A),
            kb_s.reshape(1, DB, 1, KV_B, DH_B), vb_s.reshape(1, DB, 1, KV_B, DH_B),
            lf_s.reshape(1, DB, 1, H_B))
```
